```python
import jax, jax.numpy as jnp
from jax import lax
import numpy as np

D_MODEL = 2048
BATCH = 8
SEQ = 4096
DEPTH = 4

CHUNK = 64
PLE_DIM = 256
EPS = 1e-6
MLA_HEADS = 16
Q_LORA = 512
KV_LORA = 512
QK_NOPE = 128
QK_ROPE = 64
V_DIM = 128
QK_DIM = QK_NOPE + QK_ROPE
ROPE_THETA = 10000.0
Q_BLOCK = 128
SSM_EXPAND = 2
D_INNER = SSM_EXPAND * D_MODEL
SSM_HEADDIM = 64
SSM_HEADS = D_INNER // SSM_HEADDIM
SSM_GROUPS = 8
SSM_STATE = 128
HEADS_PER_GROUP = SSM_HEADS // SSM_GROUPS
CONV_WIDTH = 4
CONV_DIM = D_INNER + 2 * SSM_GROUPS * SSM_STATE
SSD_CHUNK = CHUNK
D_FF = 4 * D_MODEL
N_BRANCH = 2
SPLIT_POINTS = (
    Q_LORA,
    Q_LORA + KV_LORA,
    Q_LORA + KV_LORA + QK_ROPE,
    Q_LORA + KV_LORA + QK_ROPE + D_INNER,
    Q_LORA + KV_LORA + QK_ROPE + D_INNER + CONV_DIM,
    Q_LORA + KV_LORA + QK_ROPE + D_INNER + CONV_DIM + SSM_HEADS,
)
D_IN_PROJ = Q_LORA + KV_LORA + QK_ROPE + D_INNER + CONV_DIM + SSM_HEADS + N_BRANCH * D_MODEL

kernel_name = 'hybrid_mla_ssd_parallel_gated_trunk'


def rms_norm(x, w):
    xf = x.astype(jnp.float32)
    y = xf * lax.rsqrt(jnp.mean(xf * xf, axis=-1, keepdims=True) + EPS)
    return (y * w.astype(jnp.float32)).astype(x.dtype)


def rope_tables(positions):
    inv_freq = 1.0 / (ROPE_THETA ** (jnp.arange(0, QK_ROPE, 2, dtype=jnp.float32) / QK_ROPE))
    ang = positions.astype(jnp.float32)[..., None] * inv_freq
    return jnp.cos(ang), jnp.sin(ang)


def apply_rope(x, cos, sin):
    xf = x.astype(jnp.float32)
    x1, x2 = jnp.split(xf, 2, axis=-1)
    c = cos[:, :, None, :]
    s = sin[:, :, None, :]
    return jnp.concatenate([x1 * c - x2 * s, x1 * s + x2 * c], axis=-1).astype(x.dtype)


def chunk_causal_attention(q, k, v):
    B, S, H, Dh = q.shape
    n_blk = S // Q_BLOCK
    scale = Dh ** -0.5
    qb = q.reshape(B, n_blk, Q_BLOCK, H, Dh).transpose(1, 0, 3, 2, 4)
    kh = k.transpose(0, 2, 1, 3)
    vh = v.transpose(0, 2, 1, 3)
    key_chunk = jnp.arange(S) // CHUNK

    def one_block(args):
        q_blk, blk = args
        s = jnp.einsum('bhqd,bhkd->bhqk', q_blk, kh, preferred_element_type=jnp.float32) * scale
        q_chunk = (blk * Q_BLOCK + jnp.arange(Q_BLOCK)) // CHUNK
        mask = key_chunk[None, :] <= q_chunk[:, None]
        s = jnp.where(mask, s, -jnp.inf)
        pr = jax.nn.softmax(s, axis=-1)
        return jnp.einsum('bhqk,bhkd->bhqd', pr.astype(vh.dtype), vh)

    o = lax.map(one_block, (qb, jnp.arange(n_blk)))
    return o.transpose(1, 0, 3, 2, 4).reshape(B, S, H, V_DIM)


def mla_branch(c_q, c_kv, k_r, cos, sin, q_a_norm_w, w_uq, kv_a_norm_w, w_ukv, q_norm_w, k_norm_w):
    B, S, _ = c_q.shape
    q = (rms_norm(c_q, q_a_norm_w) @ w_uq).reshape(B, S, MLA_HEADS, QK_DIM)
    kv = (rms_norm(c_kv, kv_a_norm_w) @ w_ukv).reshape(B, S, MLA_HEADS, QK_NOPE + V_DIM)
    k_nope, v = kv[..., :QK_NOPE], kv[..., QK_NOPE:]
    k_rope = jnp.broadcast_to(k_r[:, :, None, :], (B, S, MLA_HEADS, QK_ROPE))
    k = jnp.concatenate([k_nope, k_rope], axis=-1)
    q = rms_norm(q, q_norm_w)
    k = rms_norm(k, k_norm_w)
    q = jnp.concatenate([q[..., :QK_NOPE], apply_rope(q[..., QK_NOPE:], cos, sin)], axis=-1)
    k = jnp.concatenate([k[..., :QK_NOPE], apply_rope(k[..., QK_NOPE:], cos, sin)], axis=-1)
    o = chunk_causal_attention(q, k, v)
    return o.reshape(B, S, MLA_HEADS * V_DIM)


def causal_depthwise_conv(x, w, b):
    S = x.shape[1]
    xp = jnp.pad(x, ((0, 0), (CONV_WIDTH - 1, 0), (0, 0)))
    y = xp[:, 0:S] * w[0]
    for tap in range(1, CONV_WIDTH):
        y = y + xp[:, tap:tap + S] * w[tap]
    return y + b


def ssd_branch(xbc_raw, z, dt_raw, conv_w, conv_b, dt_bias, a_log, d_skip, ssm_norm_w):
    B, S, _ = z.shape
    G, R, P, N, T = SSM_GROUPS, HEADS_PER_GROUP, SSM_HEADDIM, SSM_STATE, SSD_CHUNK
    nc = S // T
    xbc = jax.nn.silu(causal_depthwise_conv(xbc_raw, conv_w, conv_b)).astype(jnp.float32)
    x5 = xbc[..., :D_INNER].reshape(B, nc, T, G, R, P)
    Bm = xbc[..., D_INNER:D_INNER + G * N].reshape(B, nc, T, G, N)
    Cm = xbc[..., D_INNER + G * N:].reshape(B, nc, T, G, N)
    dt = jax.nn.softplus(dt_raw.astype(jnp.float32) + dt_bias.astype(jnp.float32))
    A = -jnp.exp(a_log.astype(jnp.float32))
    dt5 = dt.reshape(B, nc, T, G, R)
    a_cum = jnp.cumsum((dt * A).reshape(B, nc, T, G, R), axis=2)
    seg = a_cum[:, :, :, None] - a_cum[:, :, None]
    tri = jnp.tril(jnp.ones((T, T), dtype=bool))
    decay = jnp.exp(jnp.where(tri[:, :, None, None], seg, -jnp.inf))
    cb = jnp.einsum('bctgn,bcsgn->bctsg', Cm, Bm)
    m = cb[..., None] * decay * dt5[:, :, None]
    y_diag = jnp.einsum('bctsgr,bcsgrp->bctgrp', m, x5)
    decay_to_end = jnp.exp(a_cum[:, :, -1:] - a_cum)
    states = jnp.einsum('bcsgn,bcsgr,bcsgrp->bcgrpn', Bm, decay_to_end * dt5, x5)
    chunk_decay = jnp.exp(a_cum[:, :, -1])

    def step(h, inp):
        st, dec = inp
        return h * dec[..., None, None] + st, h

    h0 = jnp.zeros((B, G, R, P, N), jnp.float32)
    _, prev = lax.scan(step, h0, (jnp.moveaxis(states, 1, 0), jnp.moveaxis(chunk_decay, 1, 0)))
    prev = jnp.moveaxis(prev, 0, 1)
    y_off = jnp.einsum('bctgn,bcgrpn,bctgr->bctgrp', Cm, prev, jnp.exp(a_cum))
    y = y_diag + y_off + d_skip.astype(jnp.float32).reshape(G, R)[..., None] * x5
    y = y.reshape(B, S, D_INNER) * jax.nn.silu(z.astype(jnp.float32))
    yg = y.reshape(B, S, G, D_INNER // G)
    yg = yg * lax.rsqrt(jnp.mean(yg * yg, axis=-1, keepdims=True) + EPS)
    y = yg.reshape(B, S, D_INNER) * ssm_norm_w.astype(jnp.float32)
    return y.astype(z.dtype)


def _fwd_setup_inputs(seed: int = 0) -> dict:
    key = jax.random.key(seed)
    ks = jax.random.split(key, 32)
    f32 = jnp.float32

    def nrm(k, shape, fan_in):
        return jax.random.normal(k, shape, f32) * (fan_in ** -0.5)

    def gain(k, dim):
        return 1.0 + 0.02 * jax.random.normal(k, (DEPTH, dim), f32)

    x = jax.random.normal(ks[0], (BATCH, SEQ, D_MODEL), f32)
    p = jax.random.normal(ks[1], (DEPTH, BATCH, SEQ, PLE_DIM), f32)
    start = jax.random.randint(ks[2], (BATCH, 1), 0, 16384, dtype=jnp.int32)
    positions = (start + jnp.arange(SEQ, dtype=jnp.int32)[None, :]).astype(jnp.int32)
    dt_init = jnp.exp(jax.random.uniform(ks[15], (DEPTH, SSM_HEADS), f32, np.log(1e-3), np.log(1e-1)))
    dt_bias = dt_init + jnp.log(-jnp.expm1(-dt_init))
    a_log = jnp.log(jax.random.uniform(ks[16], (DEPTH, SSM_HEADS), f32, 1.0, 16.0))
    return {
        'x': x,
        'p': p,
        'positions': positions,
        'norm_mix_w': gain(ks[3], D_MODEL),
        'w_in': nrm(ks[4], (DEPTH, D_MODEL, D_IN_PROJ), D_MODEL),
        'q_a_norm_w': gain(ks[5], Q_LORA),
        'w_uq': nrm(ks[6], (DEPTH, Q_LORA, MLA_HEADS * QK_DIM), Q_LORA),
        'kv_a_norm_w': gain(ks[7], KV_LORA),
        'w_ukv': nrm(ks[8], (DEPTH, KV_LORA, MLA_HEADS * (QK_NOPE + V_DIM)), KV_LORA),
        'q_norm_w': gain(ks[9], QK_DIM),
        'k_norm_w': gain(ks[10], QK_DIM),
        'w_o_mla': nrm(ks[11], (DEPTH, MLA_HEADS * V_DIM, D_MODEL), MLA_HEADS * V_DIM),
        'conv_w': nrm(ks[12], (DEPTH, CONV_WIDTH, CONV_DIM), CONV_WIDTH),
        'conv_b': 0.01 * jax.random.normal(ks[13], (DEPTH, CONV_DIM), f32),
        'dt_bias': dt_bias,
        'a_log': a_log,
        'd_skip': 1.0 + 0.1 * jax.random.normal(ks[17], (DEPTH, SSM_HEADS), f32),
        'ssm_norm_w': gain(ks[18], D_INNER),
        'w_o_ssm': nrm(ks[19], (DEPTH, D_INNER, D_MODEL), D_INNER),
        'w_out': nrm(ks[20], (DEPTH, D_MODEL, D_MODEL), D_MODEL),
        'norm_mlp_w': gain(ks[21], D_MODEL),
        'w_up': nrm(ks[22], (DEPTH, D_MODEL, D_FF), D_MODEL),
        'w_down': nrm(ks[23], (DEPTH, D_FF, D_MODEL), D_FF),
        'ple_norm_w': gain(ks[24], D_MODEL),
        'w_ple_gate': nrm(ks[25], (DEPTH, D_MODEL, D_MODEL), D_MODEL),
        'w_ple': nrm(ks[26], (DEPTH, PLE_DIM, D_MODEL), PLE_DIM),
    }


def _fwd_reference(x, p, positions, norm_mix_w, w_in, q_a_norm_w, w_uq, kv_a_norm_w, w_ukv,
              q_norm_w, k_norm_w, w_o_mla, conv_w, conv_b, dt_bias, a_log, d_skip,
              ssm_norm_w, w_o_ssm, w_out, norm_mlp_w, w_up, w_down, ple_norm_w,
              w_ple_gate, w_ple):
    B, S, _ = x.shape
    cos, sin = rope_tables(positions)
    for i in range(DEPTH):
        h = rms_norm(x, norm_mix_w[i])
        proj = h @ w_in[i]
        c_q, c_kv, k_r, z, xbc, dt_raw, gate_logits = jnp.split(proj, SPLIT_POINTS, axis=-1)
        y_a = mla_branch(c_q, c_kv, k_r, cos, sin, q_a_norm_w[i], w_uq[i], kv_a_norm_w[i],
                         w_ukv[i], q_norm_w[i], k_norm_w[i]) @ w_o_mla[i]
        y_b = ssd_branch(xbc, z, dt_raw, conv_w[i], conv_b[i], dt_bias[i], a_log[i],
                         d_skip[i], ssm_norm_w[i]) @ w_o_ssm[i]
        g = jax.nn.sigmoid(gate_logits.astype(jnp.float32)).reshape(B, S, N_BRANCH, D_MODEL)
        merged = (g[:, :, 0] * y_a + g[:, :, 1] * y_b).astype(x.dtype)
        x = x + (merged @ w_out[i]).astype(x.dtype)
        h2 = rms_norm(x, norm_mlp_w[i])
        x = x + (jnp.square(jax.nn.relu(h2 @ w_up[i])) @ w_down[i]).astype(x.dtype)
        ple_gate = jax.nn.sigmoid((rms_norm(x, ple_norm_w[i]) @ w_ple_gate[i]).astype(jnp.float32))
        x = x + ((p[i] @ w_ple[i]) * ple_gate).astype(x.dtype)
    return x


import jax as _jax
import jax.numpy as _jnp

TWIN_FORMAT = 'train_step'
FWD_PARAMS = ['x', 'p', 'positions', 'norm_mix_w', 'w_in', 'q_a_norm_w', 'w_uq', 'kv_a_norm_w', 'w_ukv', 'q_norm_w', 'k_norm_w', 'w_o_mla', 'conv_w', 'conv_b', 'dt_bias', 'a_log', 'd_skip', 'ssm_norm_w', 'w_o_ssm', 'w_out', 'norm_mlp_w', 'w_up', 'w_down', 'ple_norm_w', 'w_ple_gate', 'w_ple']
TWIN_WEIGHTS = ['norm_mix_w', 'w_in', 'q_a_norm_w', 'w_uq', 'kv_a_norm_w', 'w_ukv', 'q_norm_w', 'k_norm_w', 'w_o_mla', 'conv_w', 'conv_b', 'dt_bias', 'a_log', 'd_skip', 'ssm_norm_w', 'w_o_ssm', 'w_out', 'norm_mlp_w', 'w_up', 'w_down', 'ple_norm_w', 'w_ple_gate', 'w_ple']
TWIN_DIFF_INPUT = 'x'
TWIN_INPUTS = ['x', 'p', 'positions', 'norm_mix_w', 'w_in', 'q_a_norm_w', 'w_uq', 'kv_a_norm_w', 'w_ukv', 'q_norm_w', 'k_norm_w', 'w_o_mla', 'conv_w', 'conv_b', 'dt_bias', 'a_log', 'd_skip', 'ssm_norm_w', 'w_o_ssm', 'w_out', 'norm_mlp_w', 'w_up', 'w_down', 'ple_norm_w', 'w_ple_gate', 'w_ple', 'loss_target', 'm_norm_mix_w', 'm_w_in', 'm_q_a_norm_w', 'm_w_uq', 'm_kv_a_norm_w', 'm_w_ukv', 'm_q_norm_w', 'm_k_norm_w', 'm_w_o_mla', 'm_conv_w', 'm_conv_b', 'm_dt_bias', 'm_a_log', 'm_d_skip', 'm_ssm_norm_w', 'm_w_o_ssm', 'm_w_out', 'm_norm_mlp_w', 'm_w_up', 'm_w_down', 'm_ple_norm_w', 'm_w_ple_gate', 'm_w_ple', 'v_norm_mix_w', 'v_w_in', 'v_q_a_norm_w', 'v_w_uq', 'v_kv_a_norm_w', 'v_w_ukv', 'v_q_norm_w', 'v_k_norm_w', 'v_w_o_mla', 'v_conv_w', 'v_conv_b', 'v_dt_bias', 'v_a_log', 'v_d_skip', 'v_ssm_norm_w', 'v_w_o_ssm', 'v_w_out', 'v_norm_mlp_w', 'v_w_up', 'v_w_down', 'v_ple_norm_w', 'v_w_ple_gate', 'v_w_ple']
TWIN_OUTPUTS = ['loss', 'grad_x', 'grad_norm_mix_w', 'grad_w_in', 'grad_q_a_norm_w', 'grad_w_uq', 'grad_kv_a_norm_w', 'grad_w_ukv', 'grad_q_norm_w', 'grad_k_norm_w', 'grad_w_o_mla', 'grad_conv_w', 'grad_conv_b', 'grad_dt_bias', 'grad_a_log', 'grad_d_skip', 'grad_ssm_norm_w', 'grad_w_o_ssm', 'grad_w_out', 'grad_norm_mlp_w', 'grad_w_up', 'grad_w_down', 'grad_ple_norm_w', 'grad_w_ple_gate', 'grad_w_ple', 'delta_norm_mix_w', 'delta_w_in', 'delta_q_a_norm_w', 'delta_w_uq', 'delta_kv_a_norm_w', 'delta_w_ukv', 'delta_q_norm_w', 'delta_k_norm_w', 'delta_w_o_mla', 'delta_conv_w', 'delta_conv_b', 'delta_dt_bias', 'delta_a_log', 'delta_d_skip', 'delta_ssm_norm_w', 'delta_w_o_ssm', 'delta_w_out', 'delta_norm_mlp_w', 'delta_w_up', 'delta_w_down', 'delta_ple_norm_w', 'delta_w_ple_gate', 'delta_w_ple', 'new_m_norm_mix_w', 'new_m_w_in', 'new_m_q_a_norm_w', 'new_m_w_uq', 'new_m_kv_a_norm_w', 'new_m_w_ukv', 'new_m_q_norm_w', 'new_m_k_norm_w', 'new_m_w_o_mla', 'new_m_conv_w', 'new_m_conv_b', 'new_m_dt_bias', 'new_m_a_log', 'new_m_d_skip', 'new_m_ssm_norm_w', 'new_m_w_o_ssm', 'new_m_w_out', 'new_m_norm_mlp_w', 'new_m_w_up', 'new_m_w_down', 'new_m_ple_norm_w', 'new_m_w_ple_gate', 'new_m_w_ple', 'new_v_norm_mix_w', 'new_v_w_in', 'new_v_q_a_norm_w', 'new_v_w_uq', 'new_v_kv_a_norm_w', 'new_v_w_ukv', 'new_v_q_norm_w', 'new_v_k_norm_w', 'new_v_w_o_mla', 'new_v_conv_w', 'new_v_conv_b', 'new_v_dt_bias', 'new_v_a_log', 'new_v_d_skip', 'new_v_ssm_norm_w', 'new_v_w_o_ssm', 'new_v_w_out', 'new_v_norm_mlp_w', 'new_v_w_up', 'new_v_w_down', 'new_v_ple_norm_w', 'new_v_w_ple_gate', 'new_v_w_ple']
TWIN_LEAF_KINDS = {'loss': 'loss', 'grad_x': 'grad_x', 'grad_norm_mix_w': 'grad_w', 'grad_w_in': 'grad_w', 'grad_q_a_norm_w': 'grad_w', 'grad_w_uq': 'grad_w', 'grad_kv_a_norm_w': 'grad_w', 'grad_w_ukv': 'grad_w', 'grad_q_norm_w': 'grad_w', 'grad_k_norm_w': 'grad_w', 'grad_w_o_mla': 'grad_w', 'grad_conv_w': 'grad_w', 'grad_conv_b': 'grad_w', 'grad_dt_bias': 'grad_w', 'grad_a_log': 'grad_w', 'grad_d_skip': 'grad_w', 'grad_ssm_norm_w': 'grad_w', 'grad_w_o_ssm': 'grad_w', 'grad_w_out': 'grad_w', 'grad_norm_mlp_w': 'grad_w', 'grad_w_up': 'grad_w', 'grad_w_down': 'grad_w', 'grad_ple_norm_w': 'grad_w', 'grad_w_ple_gate': 'grad_w', 'grad_w_ple': 'grad_w', 'delta_norm_mix_w': 'delta_w', 'delta_w_in': 'delta_w', 'delta_q_a_norm_w': 'delta_w', 'delta_w_uq': 'delta_w', 'delta_kv_a_norm_w': 'delta_w', 'delta_w_ukv': 'delta_w', 'delta_q_norm_w': 'delta_w', 'delta_k_norm_w': 'delta_w', 'delta_w_o_mla': 'delta_w', 'delta_conv_w': 'delta_w', 'delta_conv_b': 'delta_w', 'delta_dt_bias': 'delta_w', 'delta_a_log': 'delta_w', 'delta_d_skip': 'delta_w', 'delta_ssm_norm_w': 'delta_w', 'delta_w_o_ssm': 'delta_w', 'delta_w_out': 'delta_w', 'delta_norm_mlp_w': 'delta_w', 'delta_w_up': 'delta_w', 'delta_w_down': 'delta_w', 'delta_ple_norm_w': 'delta_w', 'delta_w_ple_gate': 'delta_w', 'delta_w_ple': 'delta_w', 'new_m_norm_mix_w': 'new_m', 'new_m_w_in': 'new_m', 'new_m_q_a_norm_w': 'new_m', 'new_m_w_uq': 'new_m', 'new_m_kv_a_norm_w': 'new_m', 'new_m_w_ukv': 'new_m', 'new_m_q_norm_w': 'new_m', 'new_m_k_norm_w': 'new_m', 'new_m_w_o_mla': 'new_m', 'new_m_conv_w': 'new_m', 'new_m_conv_b': 'new_m', 'new_m_dt_bias': 'new_m', 'new_m_a_log': 'new_m', 'new_m_d_skip': 'new_m', 'new_m_ssm_norm_w': 'new_m', 'new_m_w_o_ssm': 'new_m', 'new_m_w_out': 'new_m', 'new_m_norm_mlp_w': 'new_m', 'new_m_w_up': 'new_m', 'new_m_w_down': 'new_m', 'new_m_ple_norm_w': 'new_m', 'new_m_w_ple_gate': 'new_m', 'new_m_w_ple': 'new_m', 'new_v_norm_mix_w': 'new_v', 'new_v_w_in': 'new_v', 'new_v_q_a_norm_w': 'new_v', 'new_v_w_uq': 'new_v', 'new_v_kv_a_norm_w': 'new_v', 'new_v_w_ukv': 'new_v', 'new_v_q_norm_w': 'new_v', 'new_v_k_norm_w': 'new_v', 'new_v_w_o_mla': 'new_v', 'new_v_conv_w': 'new_v', 'new_v_conv_b': 'new_v', 'new_v_dt_bias': 'new_v', 'new_v_a_log': 'new_v', 'new_v_d_skip': 'new_v', 'new_v_ssm_norm_w': 'new_v', 'new_v_w_o_ssm': 'new_v', 'new_v_w_out': 'new_v', 'new_v_norm_mlp_w': 'new_v', 'new_v_w_up': 'new_v', 'new_v_w_down': 'new_v', 'new_v_ple_norm_w': 'new_v', 'new_v_w_ple_gate': 'new_v', 'new_v_w_ple': 'new_v'}


def _forward(args):
    return _fwd_reference(*[args[k] for k in FWD_PARAMS])


def _output_shape():
    def fwd():
        inp = _fwd_setup_inputs(0)
        return _fwd_reference(*[inp[k] for k in FWD_PARAMS])
    out = _jax.eval_shape(fwd)
    return out.shape, out.dtype

N_MICROBATCH = 1
ADAM_LR = 0.001
ADAM_B1 = 0.9
ADAM_B2 = 0.999
ADAM_EPS = 1e-08
ADAM_WD = 0.01
ADAM_STEP = 10
PER_EXAMPLE_BATCH_AXIS = {'x': 0, 'p': 1, 'positions': 0, 'loss_target': 0}
SHARED_INPUTS = []
_WEIGHT_DTYPES = {'norm_mix_w': _jnp.float32, 'w_in': _jnp.float32, 'q_a_norm_w': _jnp.float32, 'w_uq': _jnp.float32, 'kv_a_norm_w': _jnp.float32, 'w_ukv': _jnp.float32, 'q_norm_w': _jnp.float32, 'k_norm_w': _jnp.float32, 'w_o_mla': _jnp.float32, 'conv_w': _jnp.float32, 'conv_b': _jnp.float32, 'dt_bias': _jnp.float32, 'a_log': _jnp.float32, 'd_skip': _jnp.float32, 'ssm_norm_w': _jnp.float32, 'w_o_ssm': _jnp.float32, 'w_out': _jnp.float32, 'norm_mlp_w': _jnp.float32, 'w_up': _jnp.float32, 'w_down': _jnp.float32, 'ple_norm_w': _jnp.float32, 'w_ple_gate': _jnp.float32, 'w_ple': _jnp.float32}
MOMENT_SCALE = {'norm_mix_w': 4.727560e+00, 'w_in': 1.696946e+00, 'q_a_norm_w': 2.057183e-01, 'w_uq': 8.587630e-02, 'kv_a_norm_w': 8.126969e+00, 'w_ukv': 2.598889e+00, 'q_norm_w': 3.827459e-01, 'k_norm_w': 3.856464e-01, 'w_o_mla': 3.595574e+00, 'conv_w': 1.761021e+00, 'conv_b': 4.375510e+00, 'dt_bias': 1.477902e+00, 'a_log': 7.785407e+00, 'd_skip': 8.309388e+00, 'ssm_norm_w': 5.272290e+00, 'w_o_ssm': 4.046005e+00, 'w_out': 5.232778e+00, 'norm_mlp_w': 4.949501e+01, 'w_up': 3.671568e+00, 'w_down': 1.383146e+01, 'ple_norm_w': 7.354161e-01, 'w_ple_gate': 5.475867e-01, 'w_ple': 3.780090e-01}


def _to_microbatches(a, axis):
    t = _jnp.moveaxis(a, axis, 0)
    t = t.reshape((N_MICROBATCH, t.shape[0] // N_MICROBATCH) + t.shape[1:])
    return _jnp.moveaxis(t, 1, axis + 1)


def setup_inputs(seed: int = 0) -> dict:
    inp = _fwd_setup_inputs(seed)
    key = _jax.random.fold_in(_jax.random.key(seed), 7919)
    shape, _ = _output_shape()
    out = dict(inp)
    out["loss_target"] = _jax.random.normal(_jax.random.fold_in(key, 0), shape, _jnp.float32)
    for i, name in enumerate(TWIN_WEIGHTS):
        w = inp[name].astype(_jnp.float32)
        if MOMENT_SCALE is None:
            s = _jnp.sqrt(_jnp.mean(_jnp.square(w)) + 1e-30)
        else:
            s = MOMENT_SCALE[name]
        km, kv = _jax.random.split(_jax.random.fold_in(key, i + 1))
        out[name] = w
        out["m_" + name] = s * _jax.random.normal(km, w.shape, _jnp.float32)
        out["v_" + name] = (s * s) * _jax.random.uniform(kv, w.shape, _jnp.float32, 0.5, 1.5)
    if N_MICROBATCH > 1:
        for name, axis in PER_EXAMPLE_BATCH_AXIS.items():
            out[name] = _to_microbatches(out[name], axis)
    return {'x': out['x'], 'p': out['p'], 'positions': out['positions'], 'norm_mix_w': out['norm_mix_w'], 'w_in': out['w_in'], 'q_a_norm_w': out['q_a_norm_w'], 'w_uq': out['w_uq'], 'kv_a_norm_w': out['kv_a_norm_w'], 'w_ukv': out['w_ukv'], 'q_norm_w': out['q_norm_w'], 'k_norm_w': out['k_norm_w'], 'w_o_mla': out['w_o_mla'], 'conv_w': out['conv_w'], 'conv_b': out['conv_b'], 'dt_bias': out['dt_bias'], 'a_log': out['a_log'], 'd_skip': out['d_skip'], 'ssm_norm_w': out['ssm_norm_w'], 'w_o_ssm': out['w_o_ssm'], 'w_out': out['w_out'], 'norm_mlp_w': out['norm_mlp_w'], 'w_up': out['w_up'], 'w_down': out['w_down'], 'ple_norm_w': out['ple_norm_w'], 'w_ple_gate': out['w_ple_gate'], 'w_ple': out['w_ple'], 'loss_target': out['loss_target'], 'm_norm_mix_w': out['m_norm_mix_w'], 'm_w_in': out['m_w_in'], 'm_q_a_norm_w': out['m_q_a_norm_w'], 'm_w_uq': out['m_w_uq'], 'm_kv_a_norm_w': out['m_kv_a_norm_w'], 'm_w_ukv': out['m_w_ukv'], 'm_q_norm_w': out['m_q_norm_w'], 'm_k_norm_w': out['m_k_norm_w'], 'm_w_o_mla': out['m_w_o_mla'], 'm_conv_w': out['m_conv_w'], 'm_conv_b': out['m_conv_b'], 'm_dt_bias': out['m_dt_bias'], 'm_a_log': out['m_a_log'], 'm_d_skip': out['m_d_skip'], 'm_ssm_norm_w': out['m_ssm_norm_w'], 'm_w_o_ssm': out['m_w_o_ssm'], 'm_w_out': out['m_w_out'], 'm_norm_mlp_w': out['m_norm_mlp_w'], 'm_w_up': out['m_w_up'], 'm_w_down': out['m_w_down'], 'm_ple_norm_w': out['m_ple_norm_w'], 'm_w_ple_gate': out['m_w_ple_gate'], 'm_w_ple': out['m_w_ple'], 'v_norm_mix_w': out['v_norm_mix_w'], 'v_w_in': out['v_w_in'], 'v_q_a_norm_w': out['v_q_a_norm_w'], 'v_w_uq': out['v_w_uq'], 'v_kv_a_norm_w': out['v_kv_a_norm_w'], 'v_w_ukv': out['v_w_ukv'], 'v_q_norm_w': out['v_q_norm_w'], 'v_k_norm_w': out['v_k_norm_w'], 'v_w_o_mla': out['v_w_o_mla'], 'v_conv_w': out['v_conv_w'], 'v_conv_b': out['v_conv_b'], 'v_dt_bias': out['v_dt_bias'], 'v_a_log': out['v_a_log'], 'v_d_skip': out['v_d_skip'], 'v_ssm_norm_w': out['v_ssm_norm_w'], 'v_w_o_ssm': out['v_w_o_ssm'], 'v_w_out': out['v_w_out'], 'v_norm_mlp_w': out['v_norm_mlp_w'], 'v_w_up': out['v_w_up'], 'v_w_down': out['v_w_down'], 'v_ple_norm_w': out['v_ple_norm_w'], 'v_w_ple_gate': out['v_w_ple_gate'], 'v_w_ple': out['v_w_ple']}


def _loss(weights, diff, rest, loss_target):
    with _jax.named_scope("forward"):
        args = {**rest, TWIN_DIFF_INPUT: diff, **{k: w.astype(_WEIGHT_DTYPES[k]) for k, w in weights.items()}}
        y = _forward(args)
    with _jax.named_scope("loss_head"):
        err = _jnp.square(y.astype(_jnp.float32) - loss_target)
        return 0.5 * _jnp.sum(_jnp.mean(err, axis=-1)) if err.ndim else 0.5 * err


def _adamw(w, g, m, v):
    m = ADAM_B1 * m + (1.0 - ADAM_B1) * g
    v = ADAM_B2 * v + (1.0 - ADAM_B2) * _jnp.square(g)
    m_hat = m / (1.0 - ADAM_B1 ** ADAM_STEP)
    v_hat = v / (1.0 - ADAM_B2 ** ADAM_STEP)
    delta = -ADAM_LR * (m_hat / (_jnp.sqrt(v_hat) + ADAM_EPS) + ADAM_WD * w)
    return delta, m, v


def reference(x, p, positions, norm_mix_w, w_in, q_a_norm_w, w_uq, kv_a_norm_w, w_ukv, q_norm_w, k_norm_w, w_o_mla, conv_w, conv_b, dt_bias, a_log, d_skip, ssm_norm_w, w_o_ssm, w_out, norm_mlp_w, w_up, w_down, ple_norm_w, w_ple_gate, w_ple, loss_target, m_norm_mix_w, m_w_in, m_q_a_norm_w, m_w_uq, m_kv_a_norm_w, m_w_ukv, m_q_norm_w, m_k_norm_w, m_w_o_mla, m_conv_w, m_conv_b, m_dt_bias, m_a_log, m_d_skip, m_ssm_norm_w, m_w_o_ssm, m_w_out, m_norm_mlp_w, m_w_up, m_w_down, m_ple_norm_w, m_w_ple_gate, m_w_ple, v_norm_mix_w, v_w_in, v_q_a_norm_w, v_w_uq, v_kv_a_norm_w, v_w_ukv, v_q_norm_w, v_k_norm_w, v_w_o_mla, v_conv_w, v_conv_b, v_dt_bias, v_a_log, v_d_skip, v_ssm_norm_w, v_w_o_ssm, v_w_out, v_norm_mlp_w, v_w_up, v_w_down, v_ple_norm_w, v_w_ple_gate, v_w_ple):
    given = dict(x=x, p=p, positions=positions, norm_mix_w=norm_mix_w, w_in=w_in, q_a_norm_w=q_a_norm_w, w_uq=w_uq, kv_a_norm_w=kv_a_norm_w, w_ukv=w_ukv, q_norm_w=q_norm_w, k_norm_w=k_norm_w, w_o_mla=w_o_mla, conv_w=conv_w, conv_b=conv_b, dt_bias=dt_bias, a_log=a_log, d_skip=d_skip, ssm_norm_w=ssm_norm_w, w_o_ssm=w_o_ssm, w_out=w_out, norm_mlp_w=norm_mlp_w, w_up=w_up, w_down=w_down, ple_norm_w=ple_norm_w, w_ple_gate=w_ple_gate, w_ple=w_ple, loss_target=loss_target, m_norm_mix_w=m_norm_mix_w, m_w_in=m_w_in, m_q_a_norm_w=m_q_a_norm_w, m_w_uq=m_w_uq, m_kv_a_norm_w=m_kv_a_norm_w, m_w_ukv=m_w_ukv, m_q_norm_w=m_q_norm_w, m_k_norm_w=m_k_norm_w, m_w_o_mla=m_w_o_mla, m_conv_w=m_conv_w, m_conv_b=m_conv_b, m_dt_bias=m_dt_bias, m_a_log=m_a_log, m_d_skip=m_d_skip, m_ssm_norm_w=m_ssm_norm_w, m_w_o_ssm=m_w_o_ssm, m_w_out=m_w_out, m_norm_mlp_w=m_norm_mlp_w, m_w_up=m_w_up, m_w_down=m_w_down, m_ple_norm_w=m_ple_norm_w, m_w_ple_gate=m_w_ple_gate, m_w_ple=m_w_ple, v_norm_mix_w=v_norm_mix_w, v_w_in=v_w_in, v_q_a_norm_w=v_q_a_norm_w, v_w_uq=v_w_uq, v_kv_a_norm_w=v_kv_a_norm_w, v_w_ukv=v_w_ukv, v_q_norm_w=v_q_norm_w, v_k_norm_w=v_k_norm_w, v_w_o_mla=v_w_o_mla, v_conv_w=v_conv_w, v_conv_b=v_conv_b, v_dt_bias=v_dt_bias, v_a_log=v_a_log, v_d_skip=v_d_skip, v_ssm_norm_w=v_ssm_norm_w, v_w_o_ssm=v_w_o_ssm, v_w_out=v_w_out, v_norm_mlp_w=v_norm_mlp_w, v_w_up=v_w_up, v_w_down=v_w_down, v_ple_norm_w=v_ple_norm_w, v_w_ple_gate=v_w_ple_gate, v_w_ple=v_w_ple)
    weights = {n: given[n] for n in TWIN_WEIGHTS}
    shared = {n: given[n] for n in SHARED_INPUTS}
    per_example = {n: given[n] for n in ['x', 'p', 'positions']}
    grad_fn = _jax.value_and_grad(_loss, argnums=(0, 1))

    def one_microbatch(ex, loss_target):
        ex = dict(ex)
        diff = ex.pop(TWIN_DIFF_INPUT)
        return grad_fn(weights, diff, {**shared, **ex}, loss_target)

    if N_MICROBATCH == 1:
        loss, (grad_w, grad_x) = one_microbatch(per_example, given["loss_target"])
    else:
        def body(carry, xs):
            loss_sum, grad_sum = carry
            l_k, (gw_k, gx_k) = one_microbatch(xs[0], xs[1])
            with _jax.named_scope("update"):
                return (loss_sum + l_k, _jax.tree.map(_jnp.add, grad_sum, gw_k)), gx_k

        init = (_jnp.zeros((), _jnp.float32), _jax.tree.map(_jnp.zeros_like, weights))
        (loss, grad_w), grad_x = _jax.lax.scan(body, init, (per_example, given["loss_target"]))
    with _jax.named_scope("update"):
        delta_w, new_m, new_v = {}, {}, {}
        for n in TWIN_WEIGHTS:
            delta_w[n], new_m[n], new_v[n] = _adamw(weights[n], grad_w[n], given["m_" + n], given["v_" + n])
    return (loss, grad_x, *[grad_w[n] for n in TWIN_WEIGHTS], *[delta_w[n] for n in TWIN_WEIGHTS],
            *[new_m[n] for n in TWIN_WEIGHTS], *[new_v[n] for n in TWIN_WEIGHTS])
```

```python
import functools
import math

import numpy as np
import jax
import jax.numpy as jnp
from jax import lax
from jax.experimental import pallas as pl
from jax.experimental.pallas import tpu as pltpu

F32 = jnp.float32
BF16 = jnp.bfloat16
MESH = pl.DeviceIdType.MESH

EPS = 1e-6
CHUNK = 64
MLA_HEADS = 16
Q_LORA = 512
KV_LORA = 512
QK_NOPE = 128
QK_ROPE = 64
V_DIM = 128
ROPE_THETA = 10000.0
SSM_HEADDIM = 64
SSM_GROUPS = 8
SSM_STATE = 128
CONV_WIDTH = 4
N_DEV = 8
ADAM_LR = 0.001
ADAM_B1 = 0.9
ADAM_B2 = 0.999
ADAM_EPS = 1e-08
ADAM_WD = 0.01
ADAM_STEP = 10

LANE = 128
SUBLANE = 8
VMEM_LIMIT = 56 * 1024 * 1024

BIG = ["w_in", "w_uq", "w_ukv", "w_o_mla", "conv_w", "w_o_ssm", "w_out", "w_up", "w_down", "w_ple_gate", "w_ple"]
ROW_SHARDED = {"w_o_mla", "w_o_ssm", "w_out", "w_down", "w_ple_gate"}
F32_ON_WIRE = {"conv_w"}
SMALL = ["norm_mix_w", "q_a_norm_w", "kv_a_norm_w", "q_norm_w", "k_norm_w", "conv_b", "dt_bias", "a_log", "d_skip",
         "ssm_norm_w", "norm_mlp_w", "ple_norm_w"]
WEIGHTS = ["norm_mix_w", "w_in", "q_a_norm_w", "w_uq", "kv_a_norm_w", "w_ukv", "q_norm_w", "k_norm_w", "w_o_mla",
           "conv_w", "conv_b", "dt_bias", "a_log", "d_skip", "ssm_norm_w", "w_o_ssm", "w_out", "norm_mlp_w", "w_up",
           "w_down", "ple_norm_w", "w_ple_gate", "w_ple"]


def _pick(dim, pref, align=LANE):
    if dim <= pref:
        return dim
    best = 0
    for t in range(align, pref + 1, align):
        if dim % t == 0:
            best = t
    return best or dim


def _params(n_axes):
    return pltpu.CompilerParams(dimension_semantics=("arbitrary",) * n_axes, vmem_limit_bytes=VMEM_LIMIT)


_DIMS = {"nn": ((1,), (0,)), "nt": ((1,), (1,)), "tn": ((0,), (0,))}


def _dg(a, b, kind):
    return lax.dot_general(a.astype(BF16), b.astype(BF16), (_DIMS[kind], ((), ())), preferred_element_type=F32)


@functools.partial(jax.custom_vjp, nondiff_argnums=(2,))
def bdot(a, b, kind):
    return _dg(a, b, kind)


def _bdot_fwd(a, b, kind):
    return _dg(a, b, kind), (a, b)


def _bdot_bwd(kind, res, g):
    a, b = res
    if kind == "nn":
        da, db = _dg(g, b, "nt"), _dg(a, g, "tn")
    elif kind == "nt":
        da, db = _dg(g, b, "nn"), _dg(g, a, "tn")
    else:
        da, db = _dg(b, g, "nt"), _dg(a, g, "nn")
    return da.astype(a.dtype), db.astype(b.dtype)


bdot.defvjp(_bdot_fwd, _bdot_bwd)


def mm(a, b, kind="nn", out_dtype=F32, add=None, name="mm", tm=1024, tn=1536, tk=1536):
    if kind == "tn":
        K, M = a.shape
    else:
        M, K = a.shape
    N = b.shape[0] if kind == "nt" else b.shape[1]
    tm, tn, tk = _pick(M, tm), _pick(N, tn), _pick(K, tk)
    nk = K // tk
    a_spec = pl.BlockSpec((tk, tm), lambda i, j, k: (k, i)) if kind == "tn" else pl.BlockSpec((tm, tk), lambda i, j, k: (i, k))
    b_spec = pl.BlockSpec((tn, tk), lambda i, j, k: (j, k)) if kind == "nt" else pl.BlockSpec((tk, tn), lambda i, j, k: (k, j))
    o_spec = pl.BlockSpec((tm, tn), lambda i, j, k: (i, j))
    has_add = add is not None

    def body(*refs):
        a_ref, b_ref = refs[0], refs[1]
        add_ref = refs[2] if has_add else None
        o_ref, acc_ref = refs[-2], refs[-1]
        k = pl.program_id(2)

        @pl.when(k == 0)
        def _():
            acc_ref[...] = jnp.zeros_like(acc_ref)

        acc_ref[...] += _dg(a_ref[...], b_ref[...], kind)

        @pl.when(k == nk - 1)
        def _():
            r = acc_ref[...]
            if has_add:
                r = r + add_ref[...].astype(F32)
            o_ref[...] = r.astype(o_ref.dtype)

    return pl.pallas_call(
        body, name=name, grid=(M // tm, N // tn, nk),
        in_specs=[a_spec, b_spec] + ([o_spec] if has_add else []), out_specs=o_spec,
        out_shape=jax.ShapeDtypeStruct((M, N), out_dtype),
        scratch_shapes=[pltpu.VMEM((tm, tn), F32)], compiler_params=_params(3),
    )(*([a, b] + ([add] if has_add else [])))


def _spec(kind, C, tile, h_outer):
    def ix(f):
        return (lambda a, b: f(a, b)) if h_outer else (lambda a, b: f(b, a))

    if kind == "r":
        return pl.BlockSpec((tile, C), ix(lambda h, i: (i, 0)))
    if kind == "h":
        return pl.BlockSpec((None, tile, C), ix(lambda h, i: (h, i, 0)))
    if kind == "c":
        return pl.BlockSpec((tile, C), ix(lambda h, i: (i, h)))
    if kind == "p":
        return pl.BlockSpec((1, C), ix(lambda h, i: (0, 0)))
    assert kind == "pc"
    return pl.BlockSpec((1, C), ix(lambda h, i: (0, h)))


def _shape(kind, C, M, H):
    return {"r": (M, C), "h": (H, M, C), "c": (M, H * C), "p": (1, C), "pc": (1, H * C)}[kind]


def _width(kind, arr, H):
    return arr.shape[-1] // H if kind in ("c", "pc") else arr.shape[-1]


def tile_fwd(fn, ins, kinds, outs, *, M, H=1, tile=256, name):
    tile = min(tile, M)
    n = len(ins)
    grid = (M // tile, H)

    def body(*refs):
        res = fn(*[r[...].astype(F32) for r in refs[:n]])
        for o, r in zip(refs[n:], res):
            o[...] = r.astype(o.dtype)

    return pl.pallas_call(
        body, name=name, grid=grid,
        in_specs=[_spec(k, _width(k, a, H), tile, False) for k, a in zip(kinds, ins)],
        out_specs=[_spec(k, C, tile, False) for k, C, _ in outs],
        out_shape=[jax.ShapeDtypeStruct(_shape(k, C, M, H), dt) for k, C, dt in outs],
        compiler_params=_params(2),
    )(*ins)


def tile_bwd(fn, ins, kinds, cts, ct_kinds, need, *, M, H=1, tile=128, h_outer=False, adds=None, name):
    tile = min(tile, M)
    n, nc = len(ins), len(cts)
    adds = adds or {}
    add_idx = sorted(adds)
    want = [j for j in range(n) if need[j]]
    grid = (H, M // tile) if h_outer else (M // tile, H)
    for j in want:
        assert not (kinds[j] == "r" and H > 1 and h_outer) and not (kinds[j] == "pc" and not h_outer)

    def body(*refs):
        in_refs, ct_refs = refs[:n], refs[n:n + nc]
        add_refs = refs[n + nc:n + nc + len(add_idx)]
        out_refs = refs[n + nc + len(add_idx):]
        h = pl.program_id(0 if h_outer else 1)
        i = pl.program_id(1 if h_outer else 0)
        _, vjp = jax.vjp(fn, *[r[...].astype(F32) for r in in_refs])
        grads = vjp(tuple(r[...].astype(F32) for r in ct_refs))
        for o, j in zip(out_refs, want):
            g = grads[j]
            if j in adds:
                g = g + add_refs[add_idx.index(j)][...].astype(F32)
            k = kinds[j]
            if k in ("h", "c") or (k == "r" and H == 1):
                o[...] = g.astype(o.dtype)
                continue
            first = {"r": h == 0, "p": jnp.logical_and(h == 0, i == 0), "pc": i == 0}[k]

            @pl.when(first)
            def _(o=o, g=g):
                o[...] = g.astype(o.dtype)

            @pl.when(jnp.logical_not(first))
            def _(o=o, g=g):
                o[...] += g.astype(o.dtype)

    specs = lambda ks, arrs: [_spec(k, _width(k, a, H), tile, h_outer) for k, a in zip(ks, arrs)]
    add_arrs = [adds[j] for j in add_idx]
    return pl.pallas_call(
        body, name=name, grid=grid,
        in_specs=specs(kinds, ins) + specs(ct_kinds, cts) + specs([kinds[j] for j in add_idx], add_arrs),
        out_specs=specs([kinds[j] for j in want], [ins[j] for j in want]),
        out_shape=[jax.ShapeDtypeStruct(ins[j].shape, F32) for j in want],
        compiler_params=_params(2),
    )(*ins, *cts, *add_arrs)


def _rms(x, w):
    return x * lax.rsqrt(jnp.mean(x * x, axis=-1, keepdims=True) + EPS) * w


def _sigmoid(x):
    return 1.0 / (1.0 + jnp.exp(-x))


def f_norm(x, w):
    return (_rms(x, w),)


def f_mla_a(cq, ckv, wq, wkv):
    return _rms(cq, wq), _rms(ckv, wkv)


def f_qknorm(xn, x1, x2, cos, sin, wn, w1, w2):
    d = xn.shape[-1] + x1.shape[-1] + x2.shape[-1]
    ms = (jnp.sum(xn * xn, axis=-1, keepdims=True) + jnp.sum(x1 * x1, axis=-1, keepdims=True)
          + jnp.sum(x2 * x2, axis=-1, keepdims=True)) / d
    r = lax.rsqrt(ms + EPS)
    b1, b2 = x1 * r * w1, x2 * r * w2
    return xn * r * wn, b1 * cos - b2 * sin, b1 * sin + b2 * cos


def f_rope_tables(pos, inv_freq):
    ang = pos * inv_freq
    return jnp.cos(ang), jnp.sin(ang)


def f_dt(dt_raw, dt_bias, a_log):
    x = dt_raw + dt_bias
    dt = jnp.maximum(x, 0.0) + jnp.log(1.0 + jnp.exp(-jnp.abs(x)))
    return dt, dt * (-jnp.exp(a_log))


def f_gated_norm(y, z, w):
    yg = y * (z * _sigmoid(z))
    return (yg * lax.rsqrt(jnp.mean(yg * yg, axis=-1, keepdims=True) + EPS) * w,)


def f_merge(g0, g1, ya, yb):
    return (_sigmoid(g0) * ya + _sigmoid(g1) * yb,)


def f_act(u):
    r = jnp.maximum(u, 0.0)
    return (r * r,)


def f_ple(x, pe, gl):
    return (x + pe * _sigmoid(gl),)


def loss_head(y, target, tile=256):
    M, D = y.shape
    tile = min(tile, M)

    def body(y_ref, t_ref, dy_ref, loss_ref):
        e = y_ref[...] - t_ref[...]
        dy_ref[...] = e * (1.0 / D)
        part = jnp.full(loss_ref.shape, 0.5 / D * jnp.sum(e * e), F32)

        @pl.when(pl.program_id(0) == 0)
        def _():
            loss_ref[...] = part

        @pl.when(pl.program_id(0) != 0)
        def _():
            loss_ref[...] += part

    row = pl.BlockSpec((tile, D), lambda i: (i, 0))
    dy, loss = pl.pallas_call(
        body, name="loss_head", grid=(M // tile,), in_specs=[row, row],
        out_specs=[row, pl.BlockSpec((SUBLANE, LANE), lambda i: (0, 0))],
        out_shape=[jax.ShapeDtypeStruct((M, D), F32), jax.ShapeDtypeStruct((SUBLANE, LANE), F32)],
        compiler_params=_params(1),
    )(y, target)
    return dy, loss[0, 0]


def _shift_down(x, k, rows):
    return x if k == 0 else jnp.where(rows >= k, pltpu.roll(x, k, 0), 0.0)


def _shift_up(x, k, rows):
    M = x.shape[0]
    return x if k == 0 else jnp.where(rows < M - k, pltpu.roll(x, M - k, 0), 0.0)


def _conv_pre(x, w_ref, b_ref, rows):
    pre = b_ref[...] + w_ref[CONV_WIDTH - 1:CONV_WIDTH, :] * x
    for k in range(1, CONV_WIDTH):
        pre = pre + w_ref[CONV_WIDTH - 1 - k:CONV_WIDTH - k, :] * _shift_down(x, k, rows)
    return pre


def conv_fwd(x, w, b, ct=LANE):
    M, C = x.shape

    def body(x_ref, w_ref, b_ref, o_ref):
        rows = lax.broadcasted_iota(jnp.int32, (M, ct), 0)
        pre = _conv_pre(x_ref[...], w_ref, b_ref, rows)
        o_ref[...] = pre * _sigmoid(pre)

    col = pl.BlockSpec((M, ct), lambda j: (0, j))
    return pl.pallas_call(
        body, name="conv_fwd", grid=(C // ct,),
        in_specs=[col, pl.BlockSpec((CONV_WIDTH, ct), lambda j: (0, j)), pl.BlockSpec((1, ct), lambda j: (0, j))],
        out_specs=col, out_shape=jax.ShapeDtypeStruct((M, C), F32), compiler_params=_params(1),
    )(x, w, b)


def conv_bwd(x, w, b, dout, ct=LANE):
    M, C = x.shape

    def body(x_ref, w_ref, b_ref, g_ref, dx_ref, dw_ref, db_ref):
        rows = lax.broadcasted_iota(jnp.int32, (M, ct), 0)
        xv = x_ref[...]
        pre = _conv_pre(xv, w_ref, b_ref, rows)
        s = _sigmoid(pre)
        dpre = g_ref[...] * (s * (1.0 + pre * (1.0 - s)))
        db_ref[...] = jnp.sum(dpre, axis=0, keepdims=True)
        dx = w_ref[CONV_WIDTH - 1:CONV_WIDTH, :] * dpre
        dw_ref[CONV_WIDTH - 1:CONV_WIDTH, :] = jnp.sum(dpre * xv, axis=0, keepdims=True)
        for k in range(1, CONV_WIDTH):
            dx = dx + w_ref[CONV_WIDTH - 1 - k:CONV_WIDTH - k, :] * _shift_up(dpre, k, rows)
            dw_ref[CONV_WIDTH - 1 - k:CONV_WIDTH - k, :] = jnp.sum(dpre * _shift_down(xv, k, rows), axis=0, keepdims=True)
        dx_ref[...] = dx.astype(dx_ref.dtype)

    col = pl.BlockSpec((M, ct), lambda j: (0, j))
    wsp = pl.BlockSpec((CONV_WIDTH, ct), lambda j: (0, j))
    bsp = pl.BlockSpec((1, ct), lambda j: (0, j))
    return pl.pallas_call(
        body, name="conv_bwd", grid=(C // ct,), in_specs=[col, wsp, bsp, col], out_specs=[col, wsp, bsp],
        out_shape=[jax.ShapeDtypeStruct((M, C), BF16), jax.ShapeDtypeStruct((CONV_WIDTH, C), F32),
                   jax.ShapeDtypeStruct((1, C), F32)],
        compiler_params=_params(1),
    )(x, w, b, dout)


def _attn_tile(q, k, v, q0):
    tq, S = q.shape[0], k.shape[0]
    shift = int(math.log2(CHUNK))
    assert 1 << shift == CHUNK
    s = bdot(q, k, "nt") * ((QK_NOPE + QK_ROPE) ** -0.5)
    q_chunk = jnp.right_shift(q0 + lax.broadcasted_iota(jnp.int32, (tq, 1), 0), shift)
    k_chunk = jnp.right_shift(lax.broadcasted_iota(jnp.int32, (1, S), 1), shift)
    s = jnp.where(k_chunk <= q_chunk, s, -jnp.inf)
    m = lax.stop_gradient(jnp.max(s, axis=-1, keepdims=True))
    e = jnp.exp(s - m)
    p = e * (1.0 / jnp.sum(e, axis=-1, keepdims=True))
    return bdot(p, v, "nn")


ATTN_BANDS = 8


def _band(S, tq):
    return max(tq, S // ATTN_BANDS)


def attn_fwd(q, k, v, tq=256):
    H, S, Dk = q.shape
    Dv = v.shape[-1]
    tq = min(tq, S)
    band = _band(S, tq)

    def body(q_ref, k_ref, v_ref, o_ref):
        q0 = pl.program_id(1) * tq
        for b in range(S // band):
            L = (b + 1) * band

            @pl.when(lax.div(q0, band) == b)
            def _(L=L):
                o = _attn_tile(q_ref[...], k_ref[0:L, :], v_ref[0:L, :], q0)
                o_ref[...] = o.astype(o_ref.dtype)

    return pl.pallas_call(
        body, name="attn_fwd", grid=(H, S // tq),
        in_specs=[pl.BlockSpec((None, tq, Dk), lambda h, i: (h, i, 0)), pl.BlockSpec((None, S, Dk), lambda h, i: (h, 0, 0)),
                  pl.BlockSpec((None, S, Dv), lambda h, i: (h, 0, 0))],
        out_specs=pl.BlockSpec((tq, Dv), lambda h, i: (i, h)),
        out_shape=jax.ShapeDtypeStruct((S, H * Dv), BF16), compiler_params=_params(2),
    )(q, k, v)


def attn_bwd(q, k, v, do, tq=128):
    H, S, Dk = q.shape
    Dv = v.shape[-1]
    tq = min(tq, S)
    band = _band(S, tq)

    def body(q_ref, k_ref, v_ref, do_ref, dq_ref, dk_ref, dv_ref):
        i = pl.program_id(1)
        q0 = i * tq

        @pl.when(i == 0)
        def _():
            dk_ref[...] = jnp.zeros_like(dk_ref)
            dv_ref[...] = jnp.zeros_like(dv_ref)

        for b in range(S // band):
            L = (b + 1) * band

            @pl.when(lax.div(q0, band) == b)
            def _(L=L):
                fn = lambda a, b_, c: _attn_tile(a, b_, c, q0)
                _, vjp = jax.vjp(fn, q_ref[...].astype(F32), k_ref[0:L, :].astype(F32), v_ref[0:L, :].astype(F32))
                dq, dk, dv = vjp(do_ref[...].astype(F32))
                dq_ref[...] = dq
                dk_ref[0:L, :] += dk
                dv_ref[0:L, :] += dv

    qs = pl.BlockSpec((None, tq, Dk), lambda h, i: (h, i, 0))
    ks = pl.BlockSpec((None, S, Dk), lambda h, i: (h, 0, 0))
    vs = pl.BlockSpec((None, S, Dv), lambda h, i: (h, 0, 0))
    return pl.pallas_call(
        body, name="attn_bwd", grid=(H, S // tq),
        in_specs=[qs, ks, vs, pl.BlockSpec((tq, Dv), lambda h, i: (i, h))], out_specs=[qs, ks, vs],
        out_shape=[jax.ShapeDtypeStruct((H, S, Dk), F32), jax.ShapeDtypeStruct((H, S, Dk), F32),
                   jax.ShapeDtypeStruct((H, S, Dv), F32)],
        compiler_params=_params(2),
    )(q, k, v, do)


def _ssd_chunk(x, Bm, Cm, a_col, a_row, dt_col, dt_row, h_prev, d_skip):
    T = x.shape[0]
    ti = lax.broadcasted_iota(jnp.int32, (T, T), 0)
    si = lax.broadcasted_iota(jnp.int32, (T, T), 1)
    tril = ti >= si
    acum_col = jnp.sum(jnp.where(tril, a_row, 0.0), axis=1, keepdims=True)
    acum_row = jnp.sum(jnp.where(ti <= si, a_col, 0.0), axis=0, keepdims=True)
    a_end = jnp.sum(a_row, axis=1, keepdims=True)
    decay = jnp.exp(jnp.where(tril, acum_col - acum_row, -jnp.inf))
    m = bdot(Cm, Bm, "nt") * decay * dt_row
    y = bdot(m, x, "nn") + bdot(Cm, h_prev, "nt") * jnp.exp(acum_col) + d_skip * x
    states = bdot(x * (jnp.exp(a_end - acum_col) * dt_col), Bm, "tn")
    return y, h_prev * jnp.exp(a_end) + states


def _ssd_specs(T, P, N, R, G, rev, nc):
    cc = (lambda c: nc - 1 - c) if rev else (lambda c: c)
    xs = pl.BlockSpec((R, T, P), lambda g, c: (g, cc(c), 0))
    bs = pl.BlockSpec((T, N), lambda g, c: (cc(c), (G * R * P) // N + g))
    cs = pl.BlockSpec((T, N), lambda g, c: (cc(c), (G * R * P) // N + G + g))
    col = pl.BlockSpec((R, T, 1), lambda g, c: (g, cc(c), 0))
    row = pl.BlockSpec((R, None, 1, T), lambda g, c: (g, cc(c), 0, 0))
    hs = pl.BlockSpec((R, None, P, N), lambda g, c: (g, cc(c), 0, 0))
    ds = pl.BlockSpec((R, 1, 1), lambda g, c: (g, 0, 0))
    return xs, bs, cs, col, row, hs, ds


def ssd_fwd(xh, xbc, a_col, a_row, dt_col, dt_row, d_skip):
    Hs, M, P = xh.shape
    G, N, T = SSM_GROUPS, SSM_STATE, CHUNK
    R, nc = Hs // G, M // T
    xs, bs, cs, col, row, hs, ds = _ssd_specs(T, P, N, R, G, False, nc)

    def body(x_ref, b_ref, c_ref, ac_ref, ar_ref, dc_ref, dr_ref, d_ref, y_ref, hp_ref, h_scr):
        @pl.when(pl.program_id(1) == 0)
        def _():
            h_scr[...] = jnp.zeros_like(h_scr)

        Bm, Cm = b_ref[...], c_ref[...]
        for r in range(R):
            hp = h_scr[r]
            hp_ref[r] = hp
            y, hn = _ssd_chunk(x_ref[r], Bm, Cm, ac_ref[r], ar_ref[r], dc_ref[r], dr_ref[r], hp, d_ref[r])
            y_ref[r] = y
            h_scr[r] = hn

    return pl.pallas_call(
        body, name="ssd_fwd", grid=(G, nc), in_specs=[xs, bs, cs, col, row, col, row, ds], out_specs=[xs, hs],
        out_shape=[jax.ShapeDtypeStruct((Hs, M, P), F32), jax.ShapeDtypeStruct((Hs, nc, P, N), F32)],
        scratch_shapes=[pltpu.VMEM((R, P, N), F32)], compiler_params=_params(2),
    )(xh, xbc, xbc, a_col, a_row, dt_col, dt_row, d_skip)


def ssd_bwd(xh, xbc, a_col, a_row, dt_col, dt_row, d_skip, h_prev, dy):
    Hs, M, P = xh.shape
    G, N, T = SSM_GROUPS, SSM_STATE, CHUNK
    R, nc = Hs // G, M // T
    xs, bs, cs, col, row, hs, ds = _ssd_specs(T, P, N, R, G, True, nc)
    gsp = pl.BlockSpec((T, N), lambda g, c: (nc - 1 - c, g))

    def body(x_ref, b_ref, c_ref, ac_ref, ar_ref, dc_ref, dr_ref, d_ref, hp_ref, dy_ref,
             dx_ref, db_ref, dcm_ref, dac_ref, dar_ref, ddc_ref, ddr_ref, dd_ref, dh_scr):
        first = pl.program_id(1) == 0

        @pl.when(first)
        def _():
            dh_scr[...] = jnp.zeros_like(dh_scr)
            dd_ref[...] = jnp.zeros_like(dd_ref)

        Bm, Cm = b_ref[...], c_ref[...]
        db, dcm = jnp.zeros_like(Bm), jnp.zeros_like(Cm)
        for r in range(R):
            _, vjp = jax.vjp(_ssd_chunk, x_ref[r], Bm, Cm, ac_ref[r], ar_ref[r], dc_ref[r], dr_ref[r], hp_ref[r], d_ref[r])
            gx, gb, gc, gac, gar, gdc, gdr, ghp, gd = vjp((dy_ref[r], dh_scr[r]))
            dx_ref[r] = gx
            db, dcm = db + gb, dcm + gc
            dac_ref[r], dar_ref[r], ddc_ref[r], ddr_ref[r] = gac, gar, gdc, gdr
            dh_scr[r] = ghp
            dd_ref[r] += gd
        db_ref[...] = db
        dcm_ref[...] = dcm

    f = lambda shape: jax.ShapeDtypeStruct(shape, F32)
    return pl.pallas_call(
        body, name="ssd_bwd", grid=(G, nc), in_specs=[xs, bs, cs, col, row, col, row, ds, hs, xs],
        out_specs=[xs, gsp, gsp, col, row, col, row, ds],
        out_shape=[f((Hs, M, P)), f((M, G * N)), f((M, G * N)), f((Hs, M, 1)), f((Hs, nc, 1, T)), f((Hs, M, 1)),
                   f((Hs, nc, 1, T)), f((Hs, 1, 1))],
        scratch_shapes=[pltpu.VMEM((R, P, N), F32)], compiler_params=_params(2),
    )(xh, xbc, xbc, a_col, a_row, dt_col, dt_row, d_skip, h_prev, dy)


def adamw(w, g, m, v, name):
    shape = w.shape
    C = shape[-1]
    R = w.size // C
    tile = _pick(R, max(SUBLANE, (1 << 19) // C), SUBLANE)
    c1 = 1.0 / (1.0 - ADAM_B1 ** ADAM_STEP)
    c2 = 1.0 / (1.0 - ADAM_B2 ** ADAM_STEP)

    def body(w_ref, g_ref, m_ref, v_ref, d_ref, nm_ref, nv_ref):
        gv = g_ref[...]
        nm = ADAM_B1 * m_ref[...] + (1.0 - ADAM_B1) * gv
        nv = ADAM_B2 * v_ref[...] + (1.0 - ADAM_B2) * (gv * gv)
        d_ref[...] = -ADAM_LR * ((nm * c1) / (jnp.sqrt(nv * c2) + ADAM_EPS) + ADAM_WD * w_ref[...])
        nm_ref[...] = nm
        nv_ref[...] = nv

    sp = pl.BlockSpec((tile, C), lambda i: (i, 0))
    outs = pl.pallas_call(
        body, name=name, grid=(R // tile,), in_specs=[sp] * 4, out_specs=[sp] * 3,
        out_shape=[jax.ShapeDtypeStruct((R, C), F32)] * 3, compiler_params=_params(1),
    )(*[t.reshape(R, C) for t in (w, g, m, v)])
    return [o.reshape(shape) for o in outs]


def add_rows(terms, out_dtype, name):
    shape = terms[0].shape
    C = shape[-1]
    R = terms[0].size // C
    tile = _pick(R, max(16, (1 << 19) // C), 16)
    n = len(terms)

    def body(*refs):
        acc = refs[0][...].astype(F32)
        for r in refs[1:n]:
            acc = acc + r[...].astype(F32)
        refs[n][...] = acc.astype(refs[n].dtype)

    sp = pl.BlockSpec((tile, C), lambda i: (i, 0))
    out = pl.pallas_call(body, name=name, grid=(R // tile,), in_specs=[sp] * n, out_specs=sp,
                         out_shape=jax.ShapeDtypeStruct((R, C), out_dtype), compiler_params=_params(1),
                         )(*[t.reshape(R, C) for t in terms])
    return out.reshape(shape)


_HBM = pl.BlockSpec(memory_space=pltpu.HBM)
_CHIP_FLIPS = ((1, 0), (0, 1), (1, 1))
AG_COPIES = 7


def _pos():
    return lax.axis_index("x"), lax.axis_index("y"), lax.axis_index("c")


def all_gather(shards, name):
    n = len(shards)

    def body(*refs):
        x_refs, out_refs = refs[:n], refs[n:2 * n]
        send_sems, recv_sems, local_sems = refs[2 * n:]
        x, y, c = _pos()
        me, sibling = (x, y, c), (x, y, 1 - c)
        chips = [(x ^ fx, y ^ fy) for fx, fy in _CHIP_FLIPS]

        def copy(t, k, block, to, src=None):
            rows = out_refs[t].at[4 * block[0] + 2 * block[1] + block[2]]
            return pltpu.make_async_remote_copy(
                src_ref=rows if src is None else src, dst_ref=rows, send_sem=send_sems.at[AG_COPIES * t + k],
                recv_sem=recv_sems.at[AG_COPIES * t + k], device_id=to, device_id_type=MESH)

        mine = [pltpu.make_async_copy(x_refs[t], out_refs[t].at[4 * x + 2 * y + c], local_sems.at[t]) for t in range(n)]
        for cp in mine:
            cp.start()
        first = []
        for j, chip in enumerate(chips):
            first += [copy(t, 1 + j, me, (*chip, c), src=x_refs[t]) for t in range(n)]
        first += [copy(t, 0, me, sibling, src=x_refs[t]) for t in range(n)]
        for cp in first:
            cp.start()
        passed = []
        for j, chip in enumerate(chips):
            for t in range(n):
                copy(t, 1 + j, (*chip, c), me).wait_recv()
                passed.append(copy(t, 4 + j, (*chip, c), sibling))
                passed[-1].start()
        for t in range(n):
            copy(t, 0, sibling, me).wait_recv()
        for j, chip in enumerate(chips):
            for t in range(n):
                copy(t, 4 + j, (*chip, 1 - c), me).wait_recv()
        for cp in first + passed:
            cp.wait_send()
        for cp in mine:
            cp.wait()

    return pl.pallas_call(
        body, name=name, out_shape=[jax.ShapeDtypeStruct((N_DEV,) + s.shape, s.dtype) for s in shards],
        in_specs=[_HBM] * n, out_specs=[_HBM] * n,
        scratch_shapes=[pltpu.SemaphoreType.DMA((AG_COPIES * n,)), pltpu.SemaphoreType.DMA((AG_COPIES * n,)),
                        pltpu.SemaphoreType.DMA((n,))],
    )(*shards)


def exchange_sibling(srcs, name):
    n = len(srcs)

    def body(*refs):
        s_refs, o_refs, send_sems, recv_sems = refs[:n], refs[n:2 * n], refs[2 * n], refs[2 * n + 1]
        x, y, c = _pos()
        cps = [pltpu.make_async_remote_copy(src_ref=s_refs[t], dst_ref=o_refs[t], send_sem=send_sems.at[t], recv_sem=recv_sems.at[t],
                                            device_id=(x, y, 1 - c), device_id_type=MESH) for t in range(n)]
        for cp in cps:
            cp.start()
        for cp in cps:
            cp.wait()

    return pl.pallas_call(
        body, name=name, out_shape=[jax.ShapeDtypeStruct(s.shape, s.dtype) for s in srcs], in_specs=[_HBM] * n, out_specs=[_HBM] * n,
        scratch_shapes=[pltpu.SemaphoreType.DMA((n,)), pltpu.SemaphoreType.DMA((n,))],
    )(*srcs)


def exchange_chips(srcs, name):
    n = len(srcs)

    def body(*refs):
        s_refs, o_refs, send_sems, recv_sems = refs[:n], refs[n:2 * n], refs[2 * n], refs[2 * n + 1]
        x, y, c = _pos()
        cps = []
        for k, (fx, fy) in enumerate(_CHIP_FLIPS):
            px, py = x ^ fx, y ^ fy
            cps += [pltpu.make_async_remote_copy(
                src_ref=s_refs[t].at[2 * px + py], dst_ref=o_refs[t].at[k], send_sem=send_sems.at[3 * t + k],
                recv_sem=recv_sems.at[3 * t + k], device_id=(px, py, c), device_id_type=MESH) for t in range(n)]
        for cp in cps:
            cp.start()
        for cp in cps:
            cp.wait()

    return pl.pallas_call(
        body, name=name, out_shape=[jax.ShapeDtypeStruct((3,) + s.shape[1:], s.dtype) for s in srcs],
        in_specs=[_HBM] * n, out_specs=[_HBM] * n,
        scratch_shapes=[pltpu.SemaphoreType.DMA((3 * n,)), pltpu.SemaphoreType.DMA((3 * n,))],
    )(*srcs)


def reduce_scatter(blocks):
    x, y, c = _pos()
    chip = 2 * x + y
    by_core = [b.reshape((4, 2) + b.shape[1:]) for b in blocks]
    mine = [lax.dynamic_index_in_dim(b, c, 1, keepdims=False) for b in by_core]
    theirs = [lax.dynamic_index_in_dim(b, 1 - c, 1, keepdims=False).astype(BF16) for b in by_core]
    got = exchange_sibling(theirs, "rs_pair_exchange")
    pair = [add_rows([m, g], BF16, "rs_pair_sum") for m, g in zip(mine, got)]
    far = exchange_chips(pair, "rs_chip_exchange")
    own = [lax.dynamic_index_in_dim(m, chip, 0, keepdims=False) for m in mine]
    own_got = [lax.dynamic_index_in_dim(g, chip, 0, keepdims=False) for g in got]
    return [add_rows([o, og, f[0], f[1], f[2]], F32, "rs_final_sum") for o, og, f in zip(own, own_got, far)]


def _full_from_gathered(name, blk):
    if name in ROW_SHARDED:
        return blk.reshape((-1,) + blk.shape[2:])
    return jnp.moveaxis(blk, 0, -2).reshape(blk.shape[1:-1] + (-1,))


def _blocks_from_full(name, full):
    if name in ROW_SHARDED:
        return full.reshape((N_DEV, -1) + full.shape[1:])
    return jnp.moveaxis(full.reshape(full.shape[:-1] + (N_DEV, -1)), -2, 0)


def gather_layer(shards, i):
    g = all_gather([shards[n][i] if n in F32_ON_WIRE else shards[n][i].astype(BF16) for n in BIG], "weights_all_gather")
    return {n: _full_from_gathered(n, blk) for n, blk in zip(BIG, g)}


def scatter_layer_grads(grads):
    return dict(zip(BIG, reduce_scatter([_blocks_from_full(n, grads[n]) for n in BIG])))


def _heads_major(a, H):
    M = a.shape[0]
    return a.reshape(M, H, -1).transpose(1, 0, 2)


def _rows_major(a):
    H, M, _ = a.shape
    return a.transpose(1, 0, 2).reshape(M, -1)


def _col_row(a, nc):
    Hs = a.shape[1]
    t = a.T
    return t[:, :, None], t.reshape(Hs, nc, 1, -1)


def _qk_w(w):
    half = QK_ROPE // 2
    return w[None, :QK_NOPE], w[None, QK_NOPE:QK_NOPE + half], w[None, QK_NOPE + half:]


def layer_fwd(x, p_i, cos, sin, W, S):
    M, D = x.shape
    H, half = MLA_HEADS, QK_ROPE // 2
    d_inner = S["ssm_norm_w"].shape[1]
    Hs = d_inner // SSM_HEADDIM
    gn = SSM_GROUPS * SSM_STATE
    nc = M // CHUNK
    sv = {"x": x}
    h = tile_fwd(f_norm, [x, S["norm_mix_w"]], ["r", "p"], [("r", D, BF16)], M=M, name="norm_mix")[0]
    proj = mm(h, W["w_in"], name="mm_in")
    o0 = 0
    segs = []
    for wdt in (Q_LORA, KV_LORA, half, half, d_inner, d_inner + 2 * gn, Hs, D, D):
        segs.append(proj[:, o0:o0 + wdt])
        o0 += wdt
    c_q, c_kv, k1, k2, z, xbc_raw, dt_raw, g0, g1 = segs
    cqn, ckvn = tile_fwd(f_mla_a, [c_q, c_kv, S["q_a_norm_w"], S["kv_a_norm_w"]], ["r", "r", "p", "p"],
                         [("r", Q_LORA, BF16), ("r", KV_LORA, BF16)], M=M, name="mla_latent_norm")
    q3 = _heads_major(mm(cqn, W["w_uq"], name="mm_uq"), H)
    kv3 = _heads_major(mm(ckvn, W["w_ukv"], name="mm_ukv"), H)
    qn, q1, q2 = q3[..., :QK_NOPE], q3[..., QK_NOPE:QK_NOPE + half], q3[..., QK_NOPE + half:]
    kn, v = kv3[..., :QK_NOPE], kv3[..., QK_NOPE:].astype(BF16)
    hk = [("h", QK_NOPE, BF16), ("h", half, BF16), ("h", half, BF16)]
    q_in = [qn, q1, q2, cos, sin, *_qk_w(S["q_norm_w"][0])]
    k_in = [kn, k1, k2, cos, sin, *_qk_w(S["k_norm_w"][0])]
    qf = jnp.concatenate(tile_fwd(f_qknorm, q_in, ["h", "h", "h", "r", "r", "p", "p", "p"], hk, M=M, H=H, name="q_norm_rope"), -1)
    kf = jnp.concatenate(tile_fwd(f_qknorm, k_in, ["h", "r", "r", "r", "r", "p", "p", "p"], hk, M=M, H=H, name="k_norm_rope"), -1)
    o = attn_fwd(qf, kf, v)
    y_a = mm(o, W["w_o_mla"], name="mm_o_mla")
    xbc = conv_fwd(xbc_raw, W["conv_w"], S["conv_b"])
    xh = _heads_major(xbc[:, :d_inner], Hs)
    dt, a = tile_fwd(f_dt, [dt_raw, S["dt_bias"], S["a_log"]], ["r", "p", "p"], [("r", Hs, F32), ("r", Hs, F32)], M=M, name="ssd_dt")
    a_col, a_row = _col_row(a, nc)
    dt_col, dt_row = _col_row(dt, nc)
    d_skip = S["d_skip"].reshape(Hs, 1, 1)
    y_h, h_prev = ssd_fwd(xh, xbc, a_col, a_row, dt_col, dt_row, d_skip)
    y_ssd = _rows_major(y_h)
    yn = tile_fwd(f_gated_norm, [y_ssd, z, S["ssm_norm_w"]], ["c", "c", "pc"], [("c", d_inner // SSM_GROUPS, BF16)],
                  M=M, H=SSM_GROUPS, name="ssd_gated_norm")[0]
    y_b = mm(yn, W["w_o_ssm"], name="mm_o_ssm")
    merged = tile_fwd(f_merge, [g0, g1, y_a, y_b], ["r"] * 4, [("r", D, BF16)], M=M, name="merge")[0]
    x1 = mm(merged, W["w_out"], add=x, name="mm_out")
    h2 = tile_fwd(f_norm, [x1, S["norm_mlp_w"]], ["r", "p"], [("r", D, BF16)], M=M, name="norm_mlp")[0]
    u = mm(h2, W["w_up"], name="mm_up")
    act = tile_fwd(f_act, [u], ["r"], [("r", u.shape[1], BF16)], M=M, tile=128, name="relu2")[0]
    x2 = mm(act, W["w_down"], add=x1, name="mm_down")
    h3 = tile_fwd(f_norm, [x2, S["ple_norm_w"]], ["r", "p"], [("r", D, BF16)], M=M, name="norm_ple")[0]
    gl = mm(h3, W["w_ple_gate"], name="mm_ple_gate")
    pe = mm(p_i, W["w_ple"], name="mm_ple")
    x3 = tile_fwd(f_ple, [x2, pe, gl], ["r"] * 3, [("r", D, F32)], M=M, name="ple_merge")[0]
    sv.update(h=h, c_q=c_q, c_kv=c_kv, k1=k1, k2=k2, z=z, xbc_raw=xbc_raw, dt_raw=dt_raw, g0=g0, g1=g1, cqn=cqn, ckvn=ckvn,
              q_in=q_in, k_in=k_in, qf=qf, kf=kf, v=v, o=o, y_a=y_a, xbc=xbc, xh=xh, a_col=a_col, a_row=a_row, dt_col=dt_col,
              dt_row=dt_row, d_skip=d_skip, h_prev=h_prev, y_ssd=y_ssd, yn=yn, y_b=y_b, merged=merged, x1=x1, h2=h2, u=u,
              act=act, x2=x2, h3=h3, gl=gl, pe=pe, p_i=p_i)
    return x3, sv


def layer_bwd(dx3, sv, cos, sin, W, S):
    M, D = dx3.shape
    H, half = MLA_HEADS, QK_ROPE // 2
    d_inner = S["ssm_norm_w"].shape[1]
    Hs = d_inner // SSM_HEADDIM
    nc = M // CHUNK
    gw, gs = {}, {}
    dx2, dpe, dgl = tile_bwd(f_ple, [sv["x2"], sv["pe"], sv["gl"]], ["r"] * 3, [dx3], ["r"], [True] * 3, M=M, name="ple_merge_bwd")
    gw["w_ple"] = mm(sv["p_i"], dpe, "tn", name="mm_ple_dw")
    dgl = dgl.astype(BF16)
    gw["w_ple_gate"] = mm(sv["h3"], dgl, "tn", name="mm_ple_gate_dw")
    dh3 = mm(dgl, W["w_ple_gate"], "nt", name="mm_ple_gate_dx")
    dx2, gs["ple_norm_w"] = tile_bwd(f_norm, [sv["x2"], S["ple_norm_w"]], ["r", "p"], [dh3], ["r"], [True, True], M=M,
                                     adds={0: dx2}, name="norm_ple_bwd")
    dx2b = dx2.astype(BF16)
    gw["w_down"] = mm(sv["act"], dx2b, "tn", name="mm_down_dw")
    dact = mm(dx2b, W["w_down"], "nt", name="mm_down_dx")
    du = tile_bwd(f_act, [sv["u"]], ["r"], [dact], ["r"], [True], M=M, tile=128, name="relu2_bwd")[0].astype(BF16)
    gw["w_up"] = mm(sv["h2"], du, "tn", name="mm_up_dw")
    dh2 = mm(du, W["w_up"], "nt", name="mm_up_dx")
    dx1, gs["norm_mlp_w"] = tile_bwd(f_norm, [sv["x1"], S["norm_mlp_w"]], ["r", "p"], [dh2], ["r"], [True, True], M=M,
                                     adds={0: dx2}, name="norm_mlp_bwd")
    dx1b = dx1.astype(BF16)
    gw["w_out"] = mm(sv["merged"], dx1b, "tn", name="mm_out_dw")
    dmerged = mm(dx1b, W["w_out"], "nt", name="mm_out_dx")
    dg0, dg1, dy_a, dy_b = tile_bwd(f_merge, [sv["g0"], sv["g1"], sv["y_a"], sv["y_b"]], ["r"] * 4, [dmerged], ["r"], [True] * 4,
                                    M=M, name="merge_bwd")
    dy_a = dy_a.astype(BF16)
    gw["w_o_mla"] = mm(sv["o"], dy_a, "tn", name="mm_o_mla_dw")
    do = mm(dy_a, W["w_o_mla"], "nt", out_dtype=BF16, name="mm_o_mla_dx")
    dqf, dkf, dv = attn_bwd(sv["qf"], sv["kf"], sv["v"], do)
    split = lambda t: [t[..., :QK_NOPE], t[..., QK_NOPE:QK_NOPE + half], t[..., QK_NOPE + half:]]
    hk = ["h", "h", "h"]
    need_q = [True, True, True, False, False, True, True, True]
    dqn, dq1, dq2, gqn, gq1, gq2 = tile_bwd(f_qknorm, sv["q_in"], ["h", "h", "h", "r", "r", "p", "p", "p"], split(dqf), hk,
                                            need_q, M=M, H=H, name="q_norm_rope_bwd")
    dkn, dk1, dk2, gkn, gk1, gk2 = tile_bwd(f_qknorm, sv["k_in"], ["h", "r", "r", "r", "r", "p", "p", "p"], split(dkf), hk,
                                            need_q, M=M, H=H, name="k_norm_rope_bwd")
    gs["q_norm_w"] = jnp.concatenate([gqn, gq1, gq2], -1)
    gs["k_norm_w"] = jnp.concatenate([gkn, gk1, gk2], -1)
    dq = _rows_major(jnp.concatenate([dqn, dq1, dq2], -1)).astype(BF16)
    dkv = _rows_major(jnp.concatenate([dkn, dv], -1)).astype(BF16)
    gw["w_uq"] = mm(sv["cqn"], dq, "tn", name="mm_uq_dw")
    gw["w_ukv"] = mm(sv["ckvn"], dkv, "tn", name="mm_ukv_dw")
    dcqn = mm(dq, W["w_uq"], "nt", name="mm_uq_dx")
    dckvn = mm(dkv, W["w_ukv"], "nt", name="mm_ukv_dx")
    dc_q, dc_kv, gs["q_a_norm_w"], gs["kv_a_norm_w"] = tile_bwd(
        f_mla_a, [sv["c_q"], sv["c_kv"], S["q_a_norm_w"], S["kv_a_norm_w"]], ["r", "r", "p", "p"], [dcqn, dckvn], ["r", "r"],
        [True] * 4, M=M, name="mla_latent_norm_bwd")
    dy_b = dy_b.astype(BF16)
    gw["w_o_ssm"] = mm(sv["yn"], dy_b, "tn", name="mm_o_ssm_dw")
    dyn = mm(dy_b, W["w_o_ssm"], "nt", name="mm_o_ssm_dx")
    dy_ssd, dz, gs["ssm_norm_w"] = tile_bwd(f_gated_norm, [sv["y_ssd"], sv["z"], S["ssm_norm_w"]], ["c", "c", "pc"], [dyn], ["c"],
                                            [True] * 3, M=M, H=SSM_GROUPS, h_outer=True, name="ssd_gated_norm_bwd")
    dxh, dB, dC, dac, dar, ddc, ddr, dd = ssd_bwd(sv["xh"], sv["xbc"], sv["a_col"], sv["a_row"], sv["dt_col"], sv["dt_row"],
                                                 sv["d_skip"], sv["h_prev"], _heads_major(dy_ssd, Hs))
    gs["d_skip"] = dd.reshape(1, Hs)
    da = (dac[:, :, 0] + dar.reshape(Hs, M)).T
    ddt = (ddc[:, :, 0] + ddr.reshape(Hs, M)).T
    ddt_raw, gs["dt_bias"], gs["a_log"] = tile_bwd(f_dt, [sv["dt_raw"], S["dt_bias"], S["a_log"]], ["r", "p", "p"], [ddt, da],
                                                   ["r", "r"], [True] * 3, M=M, name="ssd_dt_bwd")
    dxbc = jnp.concatenate([_rows_major(dxh), dB, dC], -1)
    dxbc_raw, gw["conv_w"], gs["conv_b"] = conv_bwd(sv["xbc_raw"], W["conv_w"], S["conv_b"], dxbc)
    b16 = lambda t: t.astype(BF16)
    dproj = jnp.concatenate([b16(dc_q), b16(dc_kv), b16(dk1), b16(dk2), b16(dz), dxbc_raw, b16(ddt_raw), b16(dg0), b16(dg1)], -1)
    gw["w_in"] = mm(sv["h"], dproj, "tn", name="mm_in_dw")
    dh = mm(dproj, W["w_in"], "nt", name="mm_in_dx")
    dx, gs["norm_mix_w"] = tile_bwd(f_norm, [sv["x"], S["norm_mix_w"]], ["r", "p"], [dh], ["r"], [True, True], M=M,
                                    adds={0: dx1}, name="norm_mix_bwd")
    return dx, gw, gs


def kernel(x, p, positions, norm_mix_w, w_in, q_a_norm_w, w_uq, kv_a_norm_w, w_ukv, q_norm_w, k_norm_w, w_o_mla, conv_w, conv_b, dt_bias, a_log, d_skip, ssm_norm_w, w_o_ssm, w_out, norm_mlp_w, w_up, w_down, ple_norm_w, w_ple_gate, w_ple, loss_target, m_norm_mix_w, m_w_in, m_q_a_norm_w, m_w_uq, m_kv_a_norm_w, m_w_ukv, m_q_norm_w, m_k_norm_w, m_w_o_mla, m_conv_w, m_conv_b, m_dt_bias, m_a_log, m_d_skip, m_ssm_norm_w, m_w_o_ssm, m_w_out, m_norm_mlp_w, m_w_up, m_w_down, m_ple_norm_w, m_w_ple_gate, m_w_ple, v_norm_mix_w, v_w_in, v_q_a_norm_w, v_w_uq, v_kv_a_norm_w, v_w_ukv, v_q_norm_w, v_k_norm_w, v_w_o_mla, v_conv_w, v_conv_b, v_dt_bias, v_a_log, v_d_skip, v_ssm_norm_w, v_w_o_ssm, v_w_out, v_norm_mlp_w, v_w_up, v_w_down, v_ple_norm_w, v_w_ple_gate, v_w_ple):
    env = dict(locals())
    w = {n: env[n] for n in WEIGHTS}
    mom = {n: env["m_" + n] for n in WEIGHTS}
    var = {n: env["v_" + n] for n in WEIGHTS}
    depth = w_in.shape[0]
    M, D = x.shape[1], x.shape[2]
    xs = x.reshape(M, D)
    half = QK_ROPE // 2
    inv_freq = jnp.asarray(1.0 / (ROPE_THETA ** (np.arange(0, QK_ROPE, 2, dtype=np.float32) / QK_ROPE)), F32)[None, :]
    cos, sin = tile_fwd(f_rope_tables, [positions.reshape(M, 1).astype(F32), inv_freq], ["r", "p"],
                        [("r", half, F32), ("r", half, F32)], M=M, name="rope_tables")
    shards = {n: w[n] for n in BIG}
    small = lambda i: {n: w[n][i][None, :] for n in SMALL}

    Ws, saved = [], []
    for i in range(depth):
        Wi = gather_layer(shards, i)
        xs, sv = layer_fwd(xs, p[i].reshape(M, -1).astype(BF16), cos, sin, Wi, small(i))
        Ws.append(Wi)
        saved.append(sv)
    dy, loss_part = loss_head(xs, loss_target.reshape(M, D))
    loss = lax.psum(loss_part, ("x", "y", "c"))

    big_g = [None] * depth
    small_g = [None] * depth
    for i in reversed(range(depth)):
        dy, gw, gs = layer_bwd(dy, saved[i], cos, sin, Ws[i], small(i))
        big_g[i] = scatter_layer_grads(gw)
        small_g[i] = gs

    flat_small = jnp.concatenate([small_g[i][n].reshape(-1) for i in range(depth) for n in SMALL])
    n_small = flat_small.shape[0]
    pad = (-n_small) % (SUBLANE * LANE)
    parts = all_gather([jnp.pad(flat_small, (0, pad)).reshape(-1, LANE)], "small_grads_all_gather")[0]
    tot = add_rows([parts[d] for d in range(N_DEV)], F32, "small_grads_sum").reshape(-1)[:n_small]
    grads, off = {}, 0
    per = {}
    for i in range(depth):
        for n in SMALL:
            sz = w[n].shape[1]
            per.setdefault(n, []).append(tot[off:off + sz])
            off += sz
    for n in SMALL:
        grads[n] = jnp.stack(per[n])
    for n in BIG:
        grads[n] = jnp.stack([big_g[i][n] for i in range(depth)])

    delta, new_m, new_v = {}, {}, {}
    for n in WEIGHTS:
        delta[n], new_m[n], new_v[n] = adamw(w[n], grads[n], mom[n], var[n], "adamw_" + n)
    return (loss, dy.reshape(x.shape), *[grads[n] for n in WEIGHTS], *[delta[n] for n in WEIGHTS],
            *[new_m[n] for n in WEIGHTS], *[new_v[n] for n in WEIGHTS])
```

```python
import functools
import math

import numpy as np
import jax
import jax.numpy as jnp
from jax import lax
from jax.experimental import pallas as pl
from jax.experimental.pallas import tpu as pltpu

F32 = jnp.float32
BF16 = jnp.bfloat16
MESH = pl.DeviceIdType.MESH

EPS = 1e-6
CHUNK = 64
MLA_HEADS = 16
Q_LORA = 512
KV_LORA = 512
QK_NOPE = 128
QK_ROPE = 64
V_DIM = 128
ROPE_THETA = 10000.0
SSM_HEADDIM = 64
SSM_GROUPS = 8
SSM_STATE = 128
CONV_WIDTH = 4
N_DEV = 8
ADAM_LR = 0.001
ADAM_B1 = 0.9
ADAM_B2 = 0.999
ADAM_EPS = 1e-08
ADAM_WD = 0.01
ADAM_STEP = 10

LANE = 128
SUBLANE = 8
VMEM_LIMIT = 56 * 1024 * 1024

BIG = ["w_in", "w_uq", "w_ukv", "w_o_mla", "conv_w", "w_o_ssm", "w_out", "w_up", "w_down", "w_ple_gate", "w_ple"]
ROW_SHARDED = {"w_o_mla", "w_o_ssm", "w_out", "w_down", "w_ple_gate"}
F32_ON_WIRE = {"conv_w"}
SMALL = ["norm_mix_w", "q_a_norm_w", "kv_a_norm_w", "q_norm_w", "k_norm_w", "conv_b", "dt_bias", "a_log", "d_skip",
         "ssm_norm_w", "norm_mlp_w", "ple_norm_w"]
WEIGHTS = ["norm_mix_w", "w_in", "q_a_norm_w", "w_uq", "kv_a_norm_w", "w_ukv", "q_norm_w", "k_norm_w", "w_o_mla",
           "conv_w", "conv_b", "dt_bias", "a_log", "d_skip", "ssm_norm_w", "w_o_ssm", "w_out", "norm_mlp_w", "w_up",
           "w_down", "ple_norm_w", "w_ple_gate", "w_ple"]


def _pick(dim, pref, align=LANE):
    if dim <= pref:
        return dim
    best = 0
    for t in range(align, pref + 1, align):
        if dim % t == 0:
            best = t
    return best or dim


def _params(n_axes):
    return pltpu.CompilerParams(dimension_semantics=("arbitrary",) * n_axes, vmem_limit_bytes=VMEM_LIMIT)


_DIMS = {"nn": ((1,), (0,)), "nt": ((1,), (1,)), "tn": ((0,), (0,))}


def _dg(a, b, kind):
    return lax.dot_general(a.astype(BF16), b.astype(BF16), (_DIMS[kind], ((), ())), preferred_element_type=F32)


@functools.partial(jax.custom_vjp, nondiff_argnums=(2,))
def bdot(a, b, kind):
    return _dg(a, b, kind)


def _bdot_fwd(a, b, kind):
    return _dg(a, b, kind), (a, b)


def _bdot_bwd(kind, res, g):
    a, b = res
    if kind == "nn":
        da, db = _dg(g, b, "nt"), _dg(a, g, "tn")
    elif kind == "nt":
        da, db = _dg(g, b, "nn"), _dg(g, a, "tn")
    else:
        da, db = _dg(b, g, "nt"), _dg(a, g, "nn")
    return da.astype(a.dtype), db.astype(b.dtype)


bdot.defvjp(_bdot_fwd, _bdot_bwd)


def mm(a, b, kind="nn", out_dtype=F32, add=None, name="mm", tm=1024, tn=1536, tk=1536):
    if kind == "tn":
        K, M = a.shape
    else:
        M, K = a.shape
    N = b.shape[0] if kind == "nt" else b.shape[1]
    tm, tn, tk = _pick(M, tm), _pick(N, tn), _pick(K, tk)
    nk = K // tk
    a_spec = pl.BlockSpec((tk, tm), lambda i, j, k: (k, i)) if kind == "tn" else pl.BlockSpec((tm, tk), lambda i, j, k: (i, k))
    b_spec = pl.BlockSpec((tn, tk), lambda i, j, k: (j, k)) if kind == "nt" else pl.BlockSpec((tk, tn), lambda i, j, k: (k, j))
    o_spec = pl.BlockSpec((tm, tn), lambda i, j, k: (i, j))
    has_add = add is not None

    def body(*refs):
        a_ref, b_ref = refs[0], refs[1]
        add_ref = refs[2] if has_add else None
        o_ref, acc_ref = refs[-2], refs[-1]
        k = pl.program_id(2)

        @pl.when(k == 0)
        def _():
            acc_ref[...] = jnp.zeros_like(acc_ref)

        acc_ref[...] += _dg(a_ref[...], b_ref[...], kind)

        @pl.when(k == nk - 1)
        def _():
            r = acc_ref[...]
            if has_add:
                r = r + add_ref[...].astype(F32)
            o_ref[...] = r.astype(o_ref.dtype)

    return pl.pallas_call(
        body, name=name, grid=(M // tm, N // tn, nk),
        in_specs=[a_spec, b_spec] + ([o_spec] if has_add else []), out_specs=o_spec,
        out_shape=jax.ShapeDtypeStruct((M, N), out_dtype),
        scratch_shapes=[pltpu.VMEM((tm, tn), F32)], compiler_params=_params(3),
    )(*([a, b] + ([add] if has_add else [])))


def _spec(kind, C, tile, h_outer, heads_block=None):
    def ix(f):
        return (lambda a, b: f(a, b)) if h_outer else (lambda a, b: f(b, a))

    if kind == "r":
        return pl.BlockSpec((tile, C), ix(lambda h, i: (i, 0)))
    if kind == "h":
        return pl.BlockSpec((heads_block, tile, C), ix(lambda h, i: (h, i, 0)))
    if kind == "c":
        return pl.BlockSpec((tile, C), ix(lambda h, i: (i, h)))
    if kind == "p":
        return pl.BlockSpec((1, C), ix(lambda h, i: (0, 0)))
    assert kind == "pc"
    return pl.BlockSpec((1, C), ix(lambda h, i: (0, h)))


def _shape(kind, C, M, H):
    return {"r": (M, C), "h": (H, M, C), "c": (M, H * C), "p": (1, C), "pc": (1, H * C)}[kind]


def _width(kind, arr, H):
    return arr.shape[-1] // H if kind in ("c", "pc") else arr.shape[-1]


def tile_fwd(fn, ins, kinds, outs, *, M, H=1, tile=256, all_heads=False, name):
    tile = min(tile, M)
    n = len(ins)
    grid = (M // tile, 1 if all_heads else H)
    hb = H if all_heads else None

    def body(*refs):
        res = fn(*[r[...].astype(F32) for r in refs[:n]])
        for o, r in zip(refs[n:], res):
            o[...] = r.astype(o.dtype)

    return pl.pallas_call(
        body, name=name, grid=grid,
        in_specs=[_spec(k, _width(k, a, H), tile, False, hb) for k, a in zip(kinds, ins)],
        out_specs=[_spec(k, C, tile, False, hb) for k, C, _ in outs],
        out_shape=[jax.ShapeDtypeStruct(_shape(k, C, M, H), dt) for k, C, dt in outs],
        compiler_params=_params(2),
    )(*ins)


def tile_bwd(fn, ins, kinds, cts, ct_kinds, need, *, M, H=1, tile=128, h_outer=False, all_heads=False, adds=None, gdt=None, name):
    tile = min(tile, M)
    n, nc = len(ins), len(cts)
    adds = adds or {}
    gdt = gdt or {}
    add_idx = sorted(adds)
    want = [j for j in range(n) if need[j]]
    hb = H if all_heads else None
    H = 1 if all_heads else H
    grid = (H, M // tile) if h_outer else (M // tile, H)
    for j in want:
        assert not (kinds[j] == "r" and H > 1 and h_outer) and not (kinds[j] == "pc" and not h_outer)

    def body(*refs):
        in_refs, ct_refs = refs[:n], refs[n:n + nc]
        add_refs = refs[n + nc:n + nc + len(add_idx)]
        out_refs = refs[n + nc + len(add_idx):]
        h = pl.program_id(0 if h_outer else 1)
        i = pl.program_id(1 if h_outer else 0)
        _, vjp = jax.vjp(fn, *[r[...].astype(F32) for r in in_refs])
        grads = vjp(tuple(r[...].astype(F32) for r in ct_refs))
        for o, j in zip(out_refs, want):
            g = grads[j]
            if j in adds:
                g = g + add_refs[add_idx.index(j)][...].astype(F32)
            k = kinds[j]
            if k in ("h", "c") or (k == "r" and H == 1):
                o[...] = g.astype(o.dtype)
                continue
            first = {"r": h == 0, "p": jnp.logical_and(h == 0, i == 0), "pc": i == 0}[k]

            @pl.when(first)
            def _(o=o, g=g):
                o[...] = g.astype(o.dtype)

            @pl.when(jnp.logical_not(first))
            def _(o=o, g=g):
                o[...] += g.astype(o.dtype)

    specs = lambda ks, arrs: [_spec(k, _width(k, a, H), tile, h_outer, hb) for k, a in zip(ks, arrs)]
    add_arrs = [adds[j] for j in add_idx]
    return pl.pallas_call(
        body, name=name, grid=grid,
        in_specs=specs(kinds, ins) + specs(ct_kinds, cts) + specs([kinds[j] for j in add_idx], add_arrs),
        out_specs=specs([kinds[j] for j in want], [ins[j] for j in want]),
        out_shape=[jax.ShapeDtypeStruct(ins[j].shape, gdt.get(j, F32)) for j in want],
        compiler_params=_params(2),
    )(*ins, *cts, *add_arrs)


def _rms(x, w):
    return x * lax.rsqrt(jnp.mean(x * x, axis=-1, keepdims=True) + EPS) * w


def _sigmoid(x):
    return 1.0 / (1.0 + jnp.exp(-x))


def f_norm(x, w):
    return (_rms(x, w),)


def f_mla_a(cq, ckv, wq, wkv):
    return _rms(cq, wq), _rms(ckv, wkv)


def f_qknorm(xn, x1, x2, cos, sin, wn, w1, w2):
    d = xn.shape[-1] + x1.shape[-1] + x2.shape[-1]
    ms = (jnp.sum(xn * xn, axis=-1, keepdims=True) + jnp.sum(x1 * x1, axis=-1, keepdims=True)
          + jnp.sum(x2 * x2, axis=-1, keepdims=True)) / d
    r = lax.rsqrt(ms + EPS)
    b1, b2 = x1 * r * w1, x2 * r * w2
    return xn * r * wn, b1 * cos - b2 * sin, b1 * sin + b2 * cos


def f_rope_tables(pos, inv_freq):
    ang = pos * inv_freq
    return jnp.cos(ang), jnp.sin(ang)


def f_dt(dt_raw, dt_bias, a_log):
    x = dt_raw + dt_bias
    dt = jnp.maximum(x, 0.0) + jnp.log(1.0 + jnp.exp(-jnp.abs(x)))
    return dt, dt * (-jnp.exp(a_log))


def f_gated_norm(y, z, w):
    yg = y * (z * _sigmoid(z))
    return (yg * lax.rsqrt(jnp.mean(yg * yg, axis=-1, keepdims=True) + EPS) * w,)


def f_merge(g0, g1, ya, yb):
    return (_sigmoid(g0) * ya + _sigmoid(g1) * yb,)


def f_act(u):
    r = jnp.maximum(u, 0.0)
    return (r * r,)


def f_ple(x, pe, gl):
    return (x + pe * _sigmoid(gl),)


def loss_head(y, target, tile=256):
    M, D = y.shape
    tile = min(tile, M)

    def body(y_ref, t_ref, dy_ref, loss_ref):
        e = y_ref[...] - t_ref[...]
        dy_ref[...] = e * (1.0 / D)
        part = jnp.full(loss_ref.shape, 0.5 / D * jnp.sum(e * e), F32)

        @pl.when(pl.program_id(0) == 0)
        def _():
            loss_ref[...] = part

        @pl.when(pl.program_id(0) != 0)
        def _():
            loss_ref[...] += part

    row = pl.BlockSpec((tile, D), lambda i: (i, 0))
    dy, loss = pl.pallas_call(
        body, name="loss_head", grid=(M // tile,), in_specs=[row, row],
        out_specs=[row, pl.BlockSpec((SUBLANE, LANE), lambda i: (0, 0))],
        out_shape=[jax.ShapeDtypeStruct((M, D), F32), jax.ShapeDtypeStruct((SUBLANE, LANE), F32)],
        compiler_params=_params(1),
    )(y, target)
    return dy, loss[0, 0]


def _shift_down(x, k, rows):
    return x if k == 0 else jnp.where(rows >= k, pltpu.roll(x, k, 0), 0.0)


def _shift_up(x, k, rows):
    M = x.shape[0]
    return x if k == 0 else jnp.where(rows < M - k, pltpu.roll(x, M - k, 0), 0.0)


def _conv_pre(x, w_ref, b_ref, rows):
    pre = b_ref[...] + w_ref[CONV_WIDTH - 1:CONV_WIDTH, :] * x
    for k in range(1, CONV_WIDTH):
        pre = pre + w_ref[CONV_WIDTH - 1 - k:CONV_WIDTH - k, :] * _shift_down(x, k, rows)
    return pre


def conv_fwd(x, w, b, ct=LANE):
    M, C = x.shape

    def body(x_ref, w_ref, b_ref, o_ref):
        rows = lax.broadcasted_iota(jnp.int32, (M, ct), 0)
        pre = _conv_pre(x_ref[...], w_ref, b_ref, rows)
        o_ref[...] = pre * _sigmoid(pre)

    col = pl.BlockSpec((M, ct), lambda j: (0, j))
    return pl.pallas_call(
        body, name="conv_fwd", grid=(C // ct,),
        in_specs=[col, pl.BlockSpec((CONV_WIDTH, ct), lambda j: (0, j)), pl.BlockSpec((1, ct), lambda j: (0, j))],
        out_specs=col, out_shape=jax.ShapeDtypeStruct((M, C), F32), compiler_params=_params(1),
    )(x, w, b)


def conv_bwd(x, w, b, dout, ct=LANE):
    M, C = x.shape

    def body(x_ref, w_ref, b_ref, g_ref, dx_ref, dw_ref, db_ref):
        rows = lax.broadcasted_iota(jnp.int32, (M, ct), 0)
        xv = x_ref[...]
        pre = _conv_pre(xv, w_ref, b_ref, rows)
        s = _sigmoid(pre)
        dpre = g_ref[...] * (s * (1.0 + pre * (1.0 - s)))
        db_ref[...] = jnp.sum(dpre, axis=0, keepdims=True)
        dx = w_ref[CONV_WIDTH - 1:CONV_WIDTH, :] * dpre
        dw_ref[CONV_WIDTH - 1:CONV_WIDTH, :] = jnp.sum(dpre * xv, axis=0, keepdims=True)
        for k in range(1, CONV_WIDTH):
            dx = dx + w_ref[CONV_WIDTH - 1 - k:CONV_WIDTH - k, :] * _shift_up(dpre, k, rows)
            dw_ref[CONV_WIDTH - 1 - k:CONV_WIDTH - k, :] = jnp.sum(dpre * _shift_down(xv, k, rows), axis=0, keepdims=True)
        dx_ref[...] = dx.astype(dx_ref.dtype)

    col = pl.BlockSpec((M, ct), lambda j: (0, j))
    wsp = pl.BlockSpec((CONV_WIDTH, ct), lambda j: (0, j))
    bsp = pl.BlockSpec((1, ct), lambda j: (0, j))
    return pl.pallas_call(
        body, name="conv_bwd", grid=(C // ct,), in_specs=[col, wsp, bsp, col], out_specs=[col, wsp, bsp],
        out_shape=[jax.ShapeDtypeStruct((M, C), BF16), jax.ShapeDtypeStruct((CONV_WIDTH, C), F32),
                   jax.ShapeDtypeStruct((1, C), F32)],
        compiler_params=_params(1),
    )(x, w, b, dout)


def _attn_tile(q, k, v, q0):
    tq, S = q.shape[0], k.shape[0]
    shift = int(math.log2(CHUNK))
    assert 1 << shift == CHUNK
    s = bdot(q, k, "nt") * ((QK_NOPE + QK_ROPE) ** -0.5)
    q_chunk = jnp.right_shift(q0 + lax.broadcasted_iota(jnp.int32, (tq, 1), 0), shift)
    k_chunk = jnp.right_shift(lax.broadcasted_iota(jnp.int32, (1, S), 1), shift)
    s = jnp.where(k_chunk <= q_chunk, s, -jnp.inf)
    m = lax.stop_gradient(jnp.max(s, axis=-1, keepdims=True))
    e = jnp.exp(s - m)
    p = e * (1.0 / jnp.sum(e, axis=-1, keepdims=True))
    return bdot(p, v, "nn")


ATTN_BANDS = 8


def _band(S, tq):
    return max(tq, S // ATTN_BANDS)


def attn_fwd(q, k, v, tq=256, rider=None):
    H, S, Dk = q.shape
    Dv = v.shape[-1]
    tq = min(tq, S)
    band = _band(S, tq)

    def body(q_ref, k_ref, v_ref, o_ref):
        q0 = pl.program_id(1) * tq
        for b in range(S // band):
            L = (b + 1) * band

            @pl.when(lax.div(q0, band) == b)
            def _(L=L):
                o = _attn_tile(q_ref[...], k_ref[0:L, :], v_ref[0:L, :], q0)
                o_ref[...] = o.astype(o_ref.dtype)

    grid = (H, S // tq)
    body, r_in, r_ispec, r_oshape, r_ospec, r_scr = hosted(body, 3, 1, grid, rider)
    return pl.pallas_call(
        body, name="attn_fwd", grid=grid,
        in_specs=[pl.BlockSpec((None, tq, Dk), lambda h, i: (h, i, 0)), pl.BlockSpec((None, S, Dk), lambda h, i: (h, 0, 0)),
                  pl.BlockSpec((None, S, Dv), lambda h, i: (h, 0, 0))] + r_ispec,
        out_specs=[pl.BlockSpec((tq, Dv), lambda h, i: (i, h))] + r_ospec,
        out_shape=[jax.ShapeDtypeStruct((S, H * Dv), BF16)] + r_oshape, scratch_shapes=r_scr, compiler_params=_params(2),
    )(q, k, v, *r_in)


def attn_bwd(q, k, v, do, tq=128, rider=None):
    H, S, Dk = q.shape
    Dv = v.shape[-1]
    tq = min(tq, S)
    band = _band(S, tq)

    def body(q_ref, k_ref, v_ref, do_ref, dq_ref, dk_ref, dv_ref):
        i = pl.program_id(1)
        q0 = i * tq

        @pl.when(i == 0)
        def _():
            dk_ref[...] = jnp.zeros_like(dk_ref)
            dv_ref[...] = jnp.zeros_like(dv_ref)

        for b in range(S // band):
            L = (b + 1) * band

            @pl.when(lax.div(q0, band) == b)
            def _(L=L):
                fn = lambda a, b_, c: _attn_tile(a, b_, c, q0)
                _, vjp = jax.vjp(fn, q_ref[...].astype(F32), k_ref[0:L, :].astype(F32), v_ref[0:L, :].astype(F32))
                dq, dk, dv = vjp(do_ref[...].astype(F32))
                dq_ref[...] = dq
                dk_ref[0:L, :] += dk
                dv_ref[0:L, :] += dv

    qs = pl.BlockSpec((None, tq, Dk), lambda h, i: (h, i, 0))
    ks = pl.BlockSpec((None, S, Dk), lambda h, i: (h, 0, 0))
    vs = pl.BlockSpec((None, S, Dv), lambda h, i: (h, 0, 0))
    grid = (H, S // tq)
    body, r_in, r_ispec, r_oshape, r_ospec, r_scr = hosted(body, 4, 3, grid, rider)
    return pl.pallas_call(
        body, name="attn_bwd", grid=grid,
        in_specs=[qs, ks, vs, pl.BlockSpec((tq, Dv), lambda h, i: (i, h))] + r_ispec, out_specs=[qs, ks, vs] + r_ospec,
        out_shape=[jax.ShapeDtypeStruct((H, S, Dk), F32), jax.ShapeDtypeStruct((H, S, Dk), F32),
                   jax.ShapeDtypeStruct((H, S, Dv), F32)] + r_oshape,
        scratch_shapes=r_scr, compiler_params=_params(2),
    )(q, k, v, do, *r_in)


def _ssd_chunk(x, Bm, Cm, a_col, a_row, dt_col, dt_row, h_prev, d_skip):
    T = x.shape[0]
    ti = lax.broadcasted_iota(jnp.int32, (T, T), 0)
    si = lax.broadcasted_iota(jnp.int32, (T, T), 1)
    tril = ti >= si
    acum_col = jnp.sum(jnp.where(tril, a_row, 0.0), axis=1, keepdims=True)
    acum_row = jnp.sum(jnp.where(ti <= si, a_col, 0.0), axis=0, keepdims=True)
    a_end = jnp.sum(a_row, axis=1, keepdims=True)
    decay = jnp.exp(jnp.where(tril, acum_col - acum_row, -jnp.inf))
    m = bdot(Cm, Bm, "nt") * decay * dt_row
    y = bdot(m, x, "nn") + bdot(Cm, h_prev, "nt") * jnp.exp(acum_col) + d_skip * x
    states = bdot(x * (jnp.exp(a_end - acum_col) * dt_col), Bm, "tn")
    return y, h_prev * jnp.exp(a_end) + states


def _ssd_specs(T, P, N, R, G, rev, nc):
    cc = (lambda c: nc - 1 - c) if rev else (lambda c: c)
    xs = pl.BlockSpec((R, T, P), lambda g, c: (g, cc(c), 0))
    bs = pl.BlockSpec((T, N), lambda g, c: (cc(c), (G * R * P) // N + g))
    cs = pl.BlockSpec((T, N), lambda g, c: (cc(c), (G * R * P) // N + G + g))
    col = pl.BlockSpec((R, T, 1), lambda g, c: (g, cc(c), 0))
    row = pl.BlockSpec((R, None, 1, T), lambda g, c: (g, cc(c), 0, 0))
    hs = pl.BlockSpec((R, None, P, N), lambda g, c: (g, cc(c), 0, 0))
    ds = pl.BlockSpec((R, 1, 1), lambda g, c: (g, 0, 0))
    return xs, bs, cs, col, row, hs, ds


def ssd_fwd(xh, xbc, a_col, a_row, dt_col, dt_row, d_skip, rider=None):
    Hs, M, P = xh.shape
    G, N, T = SSM_GROUPS, SSM_STATE, CHUNK
    R, nc = Hs // G, M // T
    xs, bs, cs, col, row, hs, ds = _ssd_specs(T, P, N, R, G, False, nc)

    def body(x_ref, b_ref, c_ref, ac_ref, ar_ref, dc_ref, dr_ref, d_ref, y_ref, hp_ref, h_scr):
        @pl.when(pl.program_id(1) == 0)
        def _():
            h_scr[...] = jnp.zeros_like(h_scr)

        Bm, Cm = b_ref[...], c_ref[...]
        for r in range(R):
            hp = h_scr[r]
            hp_ref[r] = hp
            y, hn = _ssd_chunk(x_ref[r], Bm, Cm, ac_ref[r], ar_ref[r], dc_ref[r], dr_ref[r], hp, d_ref[r])
            y_ref[r] = y
            h_scr[r] = hn

    body, r_in, r_ispec, r_oshape, r_ospec, r_scr = hosted(body, 8, 2, (G, nc), rider)
    return pl.pallas_call(
        body, name="ssd_fwd", grid=(G, nc), in_specs=[xs, bs, cs, col, row, col, row, ds] + r_ispec, out_specs=[xs, hs] + r_ospec,
        out_shape=[jax.ShapeDtypeStruct((Hs, M, P), F32), jax.ShapeDtypeStruct((Hs, nc, P, N), F32)] + r_oshape,
        scratch_shapes=[pltpu.VMEM((R, P, N), F32)] + r_scr, compiler_params=_params(2),
    )(xh, xbc, xbc, a_col, a_row, dt_col, dt_row, d_skip, *r_in)


def ssd_bwd(xh, xbc, a_col, a_row, dt_col, dt_row, d_skip, h_prev, dy, rider=None):
    Hs, M, P = xh.shape
    G, N, T = SSM_GROUPS, SSM_STATE, CHUNK
    R, nc = Hs // G, M // T
    xs, bs, cs, col, row, hs, ds = _ssd_specs(T, P, N, R, G, True, nc)
    gsp = pl.BlockSpec((T, N), lambda g, c: (nc - 1 - c, g))

    def body(x_ref, b_ref, c_ref, ac_ref, ar_ref, dc_ref, dr_ref, d_ref, hp_ref, dy_ref,
             dx_ref, db_ref, dcm_ref, dac_ref, dar_ref, ddc_ref, ddr_ref, dd_ref, dh_scr):
        first = pl.program_id(1) == 0

        @pl.when(first)
        def _():
            dh_scr[...] = jnp.zeros_like(dh_scr)
            dd_ref[...] = jnp.zeros_like(dd_ref)

        Bm, Cm = b_ref[...], c_ref[...]
        db, dcm = jnp.zeros_like(Bm), jnp.zeros_like(Cm)
        for r in range(R):
            _, vjp = jax.vjp(_ssd_chunk, x_ref[r], Bm, Cm, ac_ref[r], ar_ref[r], dc_ref[r], dr_ref[r], hp_ref[r], d_ref[r])
            gx, gb, gc, gac, gar, gdc, gdr, ghp, gd = vjp((dy_ref[r], dh_scr[r]))
            dx_ref[r] = gx
            db, dcm = db + gb, dcm + gc
            dac_ref[r], dar_ref[r], ddc_ref[r], ddr_ref[r] = gac, gar, gdc, gdr
            dh_scr[r] = ghp
            dd_ref[r] += gd
        db_ref[...] = db
        dcm_ref[...] = dcm

    f = lambda shape: jax.ShapeDtypeStruct(shape, F32)
    body, r_in, r_ispec, r_oshape, r_ospec, r_scr = hosted(body, 10, 8, (G, nc), rider)
    return pl.pallas_call(
        body, name="ssd_bwd", grid=(G, nc), in_specs=[xs, bs, cs, col, row, col, row, ds, hs, xs] + r_ispec,
        out_specs=[xs, gsp, gsp, col, row, col, row, ds] + r_ospec,
        out_shape=[f((Hs, M, P)), f((M, G * N)), f((M, G * N)), f((Hs, M, 1)), f((Hs, nc, 1, T)), f((Hs, M, 1)),
                   f((Hs, nc, 1, T)), f((Hs, 1, 1))] + r_oshape,
        scratch_shapes=[pltpu.VMEM((R, P, N), F32)] + r_scr, compiler_params=_params(2),
    )(xh, xbc, xbc, a_col, a_row, dt_col, dt_row, d_skip, h_prev, dy, *r_in)


def adamw(w, g, m, v, name):
    shape = w.shape
    C = shape[-1]
    R = w.size // C
    tile = _pick(R, max(SUBLANE, (1 << 19) // C), SUBLANE)
    c1 = 1.0 / (1.0 - ADAM_B1 ** ADAM_STEP)
    c2 = 1.0 / (1.0 - ADAM_B2 ** ADAM_STEP)

    def body(w_ref, g_ref, m_ref, v_ref, d_ref, nm_ref, nv_ref):
        gv = g_ref[...]
        nm = ADAM_B1 * m_ref[...] + (1.0 - ADAM_B1) * gv
        nv = ADAM_B2 * v_ref[...] + (1.0 - ADAM_B2) * (gv * gv)
        d_ref[...] = -ADAM_LR * ((nm * c1) / (jnp.sqrt(nv * c2) + ADAM_EPS) + ADAM_WD * w_ref[...])
        nm_ref[...] = nm
        nv_ref[...] = nv

    sp = pl.BlockSpec((tile, C), lambda i: (i, 0))
    outs = pl.pallas_call(
        body, name=name, grid=(R // tile,), in_specs=[sp] * 4, out_specs=[sp] * 3,
        out_shape=[jax.ShapeDtypeStruct((R, C), F32)] * 3, compiler_params=_params(1),
    )(*[t.reshape(R, C) for t in (w, g, m, v)])
    return [o.reshape(shape) for o in outs]


def add_rows(terms, out_dtype, name):
    shape = terms[0].shape
    C = shape[-1]
    R = terms[0].size // C
    tile = _pick(R, max(16, (1 << 19) // C), 16)
    n = len(terms)

    def body(*refs):
        acc = refs[0][...].astype(F32)
        for r in refs[1:n]:
            acc = acc + r[...].astype(F32)
        refs[n][...] = acc.astype(refs[n].dtype)

    sp = pl.BlockSpec((tile, C), lambda i: (i, 0))
    out = pl.pallas_call(body, name=name, grid=(R // tile,), in_specs=[sp] * n, out_specs=sp,
                         out_shape=jax.ShapeDtypeStruct((R, C), out_dtype), compiler_params=_params(1),
                         )(*[t.reshape(R, C) for t in terms])
    return out.reshape(shape)


_HBM = pl.BlockSpec(memory_space=pltpu.HBM)
_CHIP_FLIPS = ((1, 0), (0, 1), (1, 1))
AG_COPIES = 7


def _pos():
    return lax.axis_index("x"), lax.axis_index("y"), lax.axis_index("c")


def gather_rider(shards):
    n = len(shards)

    def copies(x_refs, out_refs, sems):
        send_sems, recv_sems, local_sems = sems
        x, y, c = _pos()
        me, sibling = (x, y, c), (x, y, 1 - c)
        chips = [(x ^ fx, y ^ fy) for fx, fy in _CHIP_FLIPS]

        def copy(t, k, block, to, src=None):
            rows = out_refs[t].at[4 * block[0] + 2 * block[1] + block[2]]
            return pltpu.make_async_remote_copy(
                src_ref=rows if src is None else src, dst_ref=rows, send_sem=send_sems.at[AG_COPIES * t + k],
                recv_sem=recv_sems.at[AG_COPIES * t + k], device_id=to, device_id_type=MESH)

        mine = [pltpu.make_async_copy(x_refs[t], out_refs[t].at[4 * x + 2 * y + c], local_sems.at[t]) for t in range(n)]
        first = []
        for j, chip in enumerate(chips):
            first += [copy(t, 1 + j, me, (*chip, c), src=x_refs[t]) for t in range(n)]
        first += [copy(t, 0, me, sibling, src=x_refs[t]) for t in range(n)]
        return copy, mine, first, me, sibling, chips, c

    def start(x_refs, out_refs, sems):
        _, mine, first, *_ = copies(x_refs, out_refs, sems)
        for cp in mine + first:
            cp.start()

    def finish(x_refs, out_refs, sems):
        copy, mine, first, me, sibling, chips, c = copies(x_refs, out_refs, sems)
        passed = []
        for j, chip in enumerate(chips):
            for t in range(n):
                copy(t, 1 + j, (*chip, c), me).wait_recv()
                passed.append(copy(t, 4 + j, (*chip, c), sibling))
                passed[-1].start()
        for t in range(n):
            copy(t, 0, sibling, me).wait_recv()
        for j, chip in enumerate(chips):
            for t in range(n):
                copy(t, 4 + j, (*chip, 1 - c), me).wait_recv()
        for cp in first + passed:
            cp.wait_send()
        for cp in mine:
            cp.wait()

    return dict(ins=list(shards), out_shapes=[jax.ShapeDtypeStruct((N_DEV,) + s.shape, s.dtype) for s in shards],
                scratch=[pltpu.SemaphoreType.DMA((AG_COPIES * n,)), pltpu.SemaphoreType.DMA((AG_COPIES * n,)),
                         pltpu.SemaphoreType.DMA((n,))], start=start, finish=finish)


def chips_rider(srcs):
    n = len(srcs)

    def copies(s_refs, o_refs, sems):
        send_sems, recv_sems = sems
        x, y, c = _pos()
        cps = []
        for k, (fx, fy) in enumerate(_CHIP_FLIPS):
            px, py = x ^ fx, y ^ fy
            cps += [pltpu.make_async_remote_copy(
                src_ref=s_refs[t].at[2 * px + py], dst_ref=o_refs[t].at[k], send_sem=send_sems.at[3 * t + k],
                recv_sem=recv_sems.at[3 * t + k], device_id=(px, py, c), device_id_type=MESH) for t in range(n)]
        return cps

    def start(s_refs, o_refs, sems):
        for cp in copies(s_refs, o_refs, sems):
            cp.start()

    def finish(s_refs, o_refs, sems):
        for cp in copies(s_refs, o_refs, sems):
            cp.wait()

    return dict(ins=list(srcs), out_shapes=[jax.ShapeDtypeStruct((3,) + s.shape[1:], s.dtype) for s in srcs],
                scratch=[pltpu.SemaphoreType.DMA((3 * n,)), pltpu.SemaphoreType.DMA((3 * n,))], start=start, finish=finish)


def run_rider(rider, name):
    ni = len(rider["ins"])
    no = len(rider["out_shapes"])

    def body(*refs):
        rider["start"](refs[:ni], refs[ni:ni + no], refs[ni + no:])
        rider["finish"](refs[:ni], refs[ni:ni + no], refs[ni + no:])

    return pl.pallas_call(body, name=name, out_shape=rider["out_shapes"], in_specs=[_HBM] * ni, out_specs=[_HBM] * no,
                          scratch_shapes=rider["scratch"])(*rider["ins"])


def hosted(body, n_in, n_out, grid, rider):
    if rider is None:
        return body, [], [], [], [], []
    ni, no = len(rider["ins"]), len(rider["out_shapes"])
    ns = len(rider["scratch"])

    def new_body(*refs):
        refs = list(refs)
        main_in, r_in = refs[:n_in], refs[n_in:n_in + ni]
        main_out, r_out = refs[n_in + ni:n_in + ni + n_out], refs[n_in + ni + n_out:n_in + ni + n_out + no]
        rest = refs[n_in + ni + n_out + no:]
        main_scr, r_scr = rest[:len(rest) - ns], rest[len(rest) - ns:]
        ids = [pl.program_id(a) for a in range(len(grid))]
        is_first = functools.reduce(jnp.logical_and, [i == 0 for i in ids])
        is_last = functools.reduce(jnp.logical_and, [i == g - 1 for i, g in zip(ids, grid)])

        @pl.when(is_first)
        def _():
            rider["start"](r_in, r_out, r_scr)

        body(*main_in, *main_out, *main_scr)

        @pl.when(is_last)
        def _():
            rider["finish"](r_in, r_out, r_scr)

    return new_body, rider["ins"], [_HBM] * ni, rider["out_shapes"], [_HBM] * no, rider["scratch"]


def all_gather(shards, name):
    return run_rider(gather_rider(shards), name)


def exchange_sibling(srcs, name):
    n = len(srcs)

    def body(*refs):
        s_refs, o_refs, send_sems, recv_sems = refs[:n], refs[n:2 * n], refs[2 * n], refs[2 * n + 1]
        x, y, c = _pos()
        cps = [pltpu.make_async_remote_copy(src_ref=s_refs[t], dst_ref=o_refs[t], send_sem=send_sems.at[t], recv_sem=recv_sems.at[t],
                                            device_id=(x, y, 1 - c), device_id_type=MESH) for t in range(n)]
        for cp in cps:
            cp.start()
        for cp in cps:
            cp.wait()

    return pl.pallas_call(
        body, name=name, out_shape=[jax.ShapeDtypeStruct(s.shape, s.dtype) for s in srcs], in_specs=[_HBM] * n, out_specs=[_HBM] * n,
        scratch_shapes=[pltpu.SemaphoreType.DMA((n,)), pltpu.SemaphoreType.DMA((n,))],
    )(*srcs)


def rs_pairs(blocks):
    x, y, c = _pos()
    chip = 2 * x + y
    by_core = [b.reshape((4, 2) + b.shape[1:]) for b in blocks]
    mine = [lax.dynamic_index_in_dim(b, c, 1, keepdims=False) for b in by_core]
    theirs = [lax.dynamic_index_in_dim(b, 1 - c, 1, keepdims=False).astype(BF16) for b in by_core]
    got = exchange_sibling(theirs, "rs_pair_exchange")
    pair = [add_rows([m, g], BF16, "rs_pair_sum") for m, g in zip(mine, got)]
    own = [lax.dynamic_index_in_dim(m, chip, 0, keepdims=False) for m in mine]
    own_got = [lax.dynamic_index_in_dim(g, chip, 0, keepdims=False) for g in got]
    return pair, own, own_got


def rs_finish(pairs, far):
    _, own, own_got = pairs
    return [add_rows([o, og, f[0], f[1], f[2]], F32, "rs_final_sum") for o, og, f in zip(own, own_got, far)]


def _full_from_gathered(name, blk):
    if name in ROW_SHARDED:
        return blk.reshape((-1,) + blk.shape[2:])
    return jnp.moveaxis(blk, 0, -2).reshape(blk.shape[1:-1] + (-1,))


def _blocks_from_full(name, full):
    if name in ROW_SHARDED:
        return full.reshape((N_DEV, -1) + full.shape[1:])
    return jnp.moveaxis(full.reshape(full.shape[:-1] + (N_DEV, -1)), -2, 0)


GATHER_WITH_ATTENTION = ["w_in", "w_uq", "w_o_mla"]
GATHER_WITH_SSD = [n for n in BIG if n not in GATHER_WITH_ATTENTION]
EARLY_GRADS = ["w_ple", "w_ple_gate", "w_down", "w_up", "w_out", "w_o_mla", "w_o_ssm"]
LATE_GRADS = [n for n in BIG if n not in EARLY_GRADS]


def wire_shards(shards, i, names):
    return [shards[n][i] if n in F32_ON_WIRE else shards[n][i].astype(BF16) for n in names]


def whole_weights(names, gathered):
    return {n: _full_from_gathered(n, blk) for n, blk in zip(names, gathered)}


def _heads_major(a, H):
    M = a.shape[0]
    return a.reshape(M, H, -1).transpose(1, 0, 2)


def _rows_major(a):
    H, M, _ = a.shape
    return a.transpose(1, 0, 2).reshape(M, -1)


def _col_row(a, nc):
    Hs = a.shape[1]
    t = a.T
    return t[:, :, None], t.reshape(Hs, nc, 1, -1)


def _qk_w(w):
    half = QK_ROPE // 2
    return w[None, :QK_NOPE], w[None, QK_NOPE:QK_NOPE + half], w[None, QK_NOPE + half:]


def layer_fwd(x, p_i, cos, sin, W, S, next_a=None, next_b=None):
    M, D = x.shape
    H, half = MLA_HEADS, QK_ROPE // 2
    d_inner = S["ssm_norm_w"].shape[1]
    Hs = d_inner // SSM_HEADDIM
    gn = SSM_GROUPS * SSM_STATE
    nc = M // CHUNK
    sv = {"x": x}
    h = tile_fwd(f_norm, [x, S["norm_mix_w"]], ["r", "p"], [("r", D, BF16)], M=M, name="norm_mix")[0]
    proj = mm(h, W["w_in"], name="mm_in")
    o0 = 0
    segs = []
    for wdt in (Q_LORA, KV_LORA, half, half, d_inner, d_inner + 2 * gn, Hs, D, D):
        segs.append(proj[:, o0:o0 + wdt])
        o0 += wdt
    c_q, c_kv, k1, k2, z, xbc_raw, dt_raw, g0, g1 = segs
    cqn, ckvn = tile_fwd(f_mla_a, [c_q, c_kv, S["q_a_norm_w"], S["kv_a_norm_w"]], ["r", "r", "p", "p"],
                         [("r", Q_LORA, BF16), ("r", KV_LORA, BF16)], M=M, name="mla_latent_norm")
    q3 = _heads_major(mm(cqn, W["w_uq"], name="mm_uq"), H)
    kv3 = _heads_major(mm(ckvn, W["w_ukv"], name="mm_ukv"), H)
    qn, q1, q2 = q3[..., :QK_NOPE], q3[..., QK_NOPE:QK_NOPE + half], q3[..., QK_NOPE + half:]
    kn, v = kv3[..., :QK_NOPE], kv3[..., QK_NOPE:].astype(BF16)
    hk = [("h", QK_NOPE, BF16), ("h", half, BF16), ("h", half, BF16)]
    q_in = [qn, q1, q2, cos, sin, *_qk_w(S["q_norm_w"][0])]
    k_in = [kn, k1, k2, cos, sin, *_qk_w(S["k_norm_w"][0])]
    qf = jnp.concatenate(tile_fwd(f_qknorm, q_in, ["h", "h", "h", "r", "r", "p", "p", "p"], hk, M=M, H=H, all_heads=True, name="q_norm_rope"), -1)
    kf = jnp.concatenate(tile_fwd(f_qknorm, k_in, ["h", "r", "r", "r", "r", "p", "p", "p"], hk, M=M, H=H, all_heads=True, name="k_norm_rope"), -1)
    o, *got_a = attn_fwd(qf, kf, v, rider=gather_rider(next_a) if next_a else None)
    y_a = mm(o, W["w_o_mla"], name="mm_o_mla")
    xbc = conv_fwd(xbc_raw, W["conv_w"], S["conv_b"])
    xh = _heads_major(xbc[:, :d_inner], Hs)
    dt, a = tile_fwd(f_dt, [dt_raw, S["dt_bias"], S["a_log"]], ["r", "p", "p"], [("r", Hs, F32), ("r", Hs, F32)], M=M, name="ssd_dt")
    a_col, a_row = _col_row(a, nc)
    dt_col, dt_row = _col_row(dt, nc)
    d_skip = S["d_skip"].reshape(Hs, 1, 1)
    y_h, h_prev, *got_b = ssd_fwd(xh, xbc, a_col, a_row, dt_col, dt_row, d_skip, rider=gather_rider(next_b) if next_b else None)
    y_ssd = _rows_major(y_h)
    yn = tile_fwd(f_gated_norm, [y_ssd, z, S["ssm_norm_w"]], ["c", "c", "pc"], [("c", d_inner // SSM_GROUPS, BF16)],
                  M=M, H=SSM_GROUPS, name="ssd_gated_norm")[0]
    y_b = mm(yn, W["w_o_ssm"], name="mm_o_ssm")
    merged = tile_fwd(f_merge, [g0, g1, y_a, y_b], ["r"] * 4, [("r", D, BF16)], M=M, name="merge")[0]
    x1 = mm(merged, W["w_out"], add=x, name="mm_out")
    h2 = tile_fwd(f_norm, [x1, S["norm_mlp_w"]], ["r", "p"], [("r", D, BF16)], M=M, name="norm_mlp")[0]
    u = mm(h2, W["w_up"], name="mm_up")
    act = tile_fwd(f_act, [u], ["r"], [("r", u.shape[1], BF16)], M=M, tile=128, name="relu2")[0]
    x2 = mm(act, W["w_down"], add=x1, name="mm_down")
    h3 = tile_fwd(f_norm, [x2, S["ple_norm_w"]], ["r", "p"], [("r", D, BF16)], M=M, name="norm_ple")[0]
    gl = mm(h3, W["w_ple_gate"], name="mm_ple_gate")
    pe = mm(p_i, W["w_ple"], name="mm_ple")
    x3 = tile_fwd(f_ple, [x2, pe, gl], ["r"] * 3, [("r", D, F32)], M=M, name="ple_merge")[0]
    sv.update(h=h, c_q=c_q, c_kv=c_kv, k1=k1, k2=k2, z=z, xbc_raw=xbc_raw, dt_raw=dt_raw, g0=g0, g1=g1, cqn=cqn, ckvn=ckvn,
              q_in=q_in, k_in=k_in, qf=qf, kf=kf, v=v, o=o, y_a=y_a, xbc=xbc, xh=xh, a_col=a_col, a_row=a_row, dt_col=dt_col,
              dt_row=dt_row, d_skip=d_skip, h_prev=h_prev, y_ssd=y_ssd, yn=yn, y_b=y_b, merged=merged, x1=x1, h2=h2, u=u,
              act=act, x2=x2, h3=h3, gl=gl, pe=pe, p_i=p_i)
    return x3, sv, got_a, got_b


def layer_bwd(dx3, sv, cos, sin, W, S, pending=None):
    M, D = dx3.shape
    H, half = MLA_HEADS, QK_ROPE // 2
    d_inner = S["ssm_norm_w"].shape[1]
    Hs = d_inner // SSM_HEADDIM
    nc = M // CHUNK
    gw, gs = {}, {}
    dx2, dpe, dgl = tile_bwd(f_ple, [sv["x2"], sv["pe"], sv["gl"]], ["r"] * 3, [dx3], ["r"], [True] * 3, M=M,
                             gdt={1: BF16, 2: BF16}, name="ple_merge_bwd")
    gw["w_ple"] = mm(sv["p_i"], dpe, "tn", name="mm_ple_dw")
    gw["w_ple_gate"] = mm(sv["h3"], dgl, "tn", name="mm_ple_gate_dw")
    dh3 = mm(dgl, W["w_ple_gate"], "nt", name="mm_ple_gate_dx")
    dx2, gs["ple_norm_w"] = tile_bwd(f_norm, [sv["x2"], S["ple_norm_w"]], ["r", "p"], [dh3], ["r"], [True, True], M=M,
                                     adds={0: dx2}, name="norm_ple_bwd")
    dx2b = dx2.astype(BF16)
    gw["w_down"] = mm(sv["act"], dx2b, "tn", name="mm_down_dw")
    dact = mm(dx2b, W["w_down"], "nt", name="mm_down_dx")
    du = tile_bwd(f_act, [sv["u"]], ["r"], [dact], ["r"], [True], M=M, tile=128, gdt={0: BF16}, name="relu2_bwd")[0]
    gw["w_up"] = mm(sv["h2"], du, "tn", name="mm_up_dw")
    dh2 = mm(du, W["w_up"], "nt", name="mm_up_dx")
    dx1, gs["norm_mlp_w"] = tile_bwd(f_norm, [sv["x1"], S["norm_mlp_w"]], ["r", "p"], [dh2], ["r"], [True, True], M=M,
                                     adds={0: dx2}, name="norm_mlp_bwd")
    dx1b = dx1.astype(BF16)
    gw["w_out"] = mm(sv["merged"], dx1b, "tn", name="mm_out_dw")
    dmerged = mm(dx1b, W["w_out"], "nt", name="mm_out_dx")
    dg0, dg1, dy_a, dy_b = tile_bwd(f_merge, [sv["g0"], sv["g1"], sv["y_a"], sv["y_b"]], ["r"] * 4, [dmerged], ["r"], [True] * 4,
                                    M=M, gdt={0: BF16, 1: BF16, 2: BF16, 3: BF16}, name="merge_bwd")
    gw["w_o_mla"] = mm(sv["o"], dy_a, "tn", name="mm_o_mla_dw")
    do = mm(dy_a, W["w_o_mla"], "nt", out_dtype=BF16, name="mm_o_mla_dx")
    gw["w_o_ssm"] = mm(sv["yn"], dy_b, "tn", name="mm_o_ssm_dw")
    early = rs_pairs([_blocks_from_full(n, gw[n]) for n in EARLY_GRADS])
    dqf, dkf, dv, *far = attn_bwd(sv["qf"], sv["kf"], sv["v"], do, rider=chips_rider(early[0]))
    red_early = dict(zip(EARLY_GRADS, rs_finish(early, far)))
    split = lambda t: [t[..., :QK_NOPE], t[..., QK_NOPE:QK_NOPE + half], t[..., QK_NOPE + half:]]
    hk = ["h", "h", "h"]
    need_q = [True, True, True, False, False, True, True, True]
    dqn, dq1, dq2, gqn, gq1, gq2 = tile_bwd(f_qknorm, sv["q_in"], ["h", "h", "h", "r", "r", "p", "p", "p"], split(dqf), hk,
                                            need_q, M=M, H=H, all_heads=True, gdt={0: BF16, 1: BF16, 2: BF16}, name="q_norm_rope_bwd")
    dkn, dk1, dk2, gkn, gk1, gk2 = tile_bwd(f_qknorm, sv["k_in"], ["h", "r", "r", "r", "r", "p", "p", "p"], split(dkf), hk,
                                            need_q, M=M, H=H, all_heads=True, gdt={0: BF16, 1: BF16, 2: BF16}, name="k_norm_rope_bwd")
    gs["q_norm_w"] = jnp.concatenate([gqn, gq1, gq2], -1)
    gs["k_norm_w"] = jnp.concatenate([gkn, gk1, gk2], -1)
    dq = _rows_major(jnp.concatenate([dqn, dq1, dq2], -1))
    dkv = _rows_major(jnp.concatenate([dkn, dv.astype(BF16)], -1))
    gw["w_uq"] = mm(sv["cqn"], dq, "tn", name="mm_uq_dw")
    gw["w_ukv"] = mm(sv["ckvn"], dkv, "tn", name="mm_ukv_dw")
    dcqn = mm(dq, W["w_uq"], "nt", name="mm_uq_dx")
    dckvn = mm(dkv, W["w_ukv"], "nt", name="mm_ukv_dx")
    dc_q, dc_kv, gs["q_a_norm_w"], gs["kv_a_norm_w"] = tile_bwd(
        f_mla_a, [sv["c_q"], sv["c_kv"], S["q_a_norm_w"], S["kv_a_norm_w"]], ["r", "r", "p", "p"], [dcqn, dckvn], ["r", "r"],
        [True] * 4, M=M, gdt={0: BF16, 1: BF16}, name="mla_latent_norm_bwd")
    dyn = mm(dy_b, W["w_o_ssm"], "nt", name="mm_o_ssm_dx")
    dy_ssd, dz, gs["ssm_norm_w"] = tile_bwd(f_gated_norm, [sv["y_ssd"], sv["z"], S["ssm_norm_w"]], ["c", "c", "pc"], [dyn], ["c"],
                                            [True] * 3, M=M, H=SSM_GROUPS, h_outer=True, gdt={1: BF16}, name="ssd_gated_norm_bwd")
    dxh, dB, dC, dac, dar, ddc, ddr, dd, *far = ssd_bwd(sv["xh"], sv["xbc"], sv["a_col"], sv["a_row"], sv["dt_col"], sv["dt_row"],
                                                       sv["d_skip"], sv["h_prev"], _heads_major(dy_ssd, Hs),
                                                       rider=chips_rider(pending[0]) if pending else None)
    red_pending = rs_finish(pending, far) if pending else None
    gs["d_skip"] = dd.reshape(1, Hs)
    da = (dac[:, :, 0] + dar.reshape(Hs, M)).T
    ddt = (ddc[:, :, 0] + ddr.reshape(Hs, M)).T
    ddt_raw, gs["dt_bias"], gs["a_log"] = tile_bwd(f_dt, [sv["dt_raw"], S["dt_bias"], S["a_log"]], ["r", "p", "p"], [ddt, da],
                                                   ["r", "r"], [True] * 3, M=M, gdt={0: BF16}, name="ssd_dt_bwd")
    dxbc = jnp.concatenate([_rows_major(dxh), dB, dC], -1)
    dxbc_raw, gw["conv_w"], gs["conv_b"] = conv_bwd(sv["xbc_raw"], W["conv_w"], S["conv_b"], dxbc)
    dproj = jnp.concatenate([dc_q, dc_kv, dk1, dk2, dz, dxbc_raw, ddt_raw, dg0, dg1], -1)
    gw["w_in"] = mm(sv["h"], dproj, "tn", name="mm_in_dw")
    dh = mm(dproj, W["w_in"], "nt", name="mm_in_dx")
    dx, gs["norm_mix_w"] = tile_bwd(f_norm, [sv["x"], S["norm_mix_w"]], ["r", "p"], [dh], ["r"], [True, True], M=M,
                                    adds={0: dx1}, name="norm_mix_bwd")
    late = rs_pairs([_blocks_from_full(n, gw[n]) for n in LATE_GRADS])
    return dx, red_early, red_pending, late, gs


def kernel(x, p, positions, norm_mix_w, w_in, q_a_norm_w, w_uq, kv_a_norm_w, w_ukv, q_norm_w, k_norm_w, w_o_mla, conv_w, conv_b, dt_bias, a_log, d_skip, ssm_norm_w, w_o_ssm, w_out, norm_mlp_w, w_up, w_down, ple_norm_w, w_ple_gate, w_ple, loss_target, m_norm_mix_w, m_w_in, m_q_a_norm_w, m_w_uq, m_kv_a_norm_w, m_w_ukv, m_q_norm_w, m_k_norm_w, m_w_o_mla, m_conv_w, m_conv_b, m_dt_bias, m_a_log, m_d_skip, m_ssm_norm_w, m_w_o_ssm, m_w_out, m_norm_mlp_w, m_w_up, m_w_down, m_ple_norm_w, m_w_ple_gate, m_w_ple, v_norm_mix_w, v_w_in, v_q_a_norm_w, v_w_uq, v_kv_a_norm_w, v_w_ukv, v_q_norm_w, v_k_norm_w, v_w_o_mla, v_conv_w, v_conv_b, v_dt_bias, v_a_log, v_d_skip, v_ssm_norm_w, v_w_o_ssm, v_w_out, v_norm_mlp_w, v_w_up, v_w_down, v_ple_norm_w, v_w_ple_gate, v_w_ple):
    env = dict(locals())
    w = {n: env[n] for n in WEIGHTS}
    mom = {n: env["m_" + n] for n in WEIGHTS}
    var = {n: env["v_" + n] for n in WEIGHTS}
    depth = w_in.shape[0]
    M, D = x.shape[1], x.shape[2]
    xs = x.reshape(M, D)
    half = QK_ROPE // 2
    inv_freq = jnp.asarray(1.0 / (ROPE_THETA ** (np.arange(0, QK_ROPE, 2, dtype=np.float32) / QK_ROPE)), F32)[None, :]
    cos, sin = tile_fwd(f_rope_tables, [positions.reshape(M, 1).astype(F32), inv_freq], ["r", "p"],
                        [("r", half, F32), ("r", half, F32)], M=M, name="rope_tables")
    shards = {n: w[n] for n in BIG}
    small = lambda i: {n: w[n][i][None, :] for n in SMALL}

    Ws, saved = [], []
    got_a = all_gather(wire_shards(shards, 0, GATHER_WITH_ATTENTION), "weights_all_gather_a")
    got_b = all_gather(wire_shards(shards, 0, GATHER_WITH_SSD), "weights_all_gather_b")
    for i in range(depth):
        Wi = {**whole_weights(GATHER_WITH_ATTENTION, got_a), **whole_weights(GATHER_WITH_SSD, got_b)}
        nxt = i + 1 < depth
        xs, sv, got_a, got_b = layer_fwd(xs, p[i].reshape(M, -1).astype(BF16), cos, sin, Wi, small(i),
                                        wire_shards(shards, i + 1, GATHER_WITH_ATTENTION) if nxt else None,
                                        wire_shards(shards, i + 1, GATHER_WITH_SSD) if nxt else None)
        Ws.append(Wi)
        saved.append(sv)
    dy, loss_part = loss_head(xs, loss_target.reshape(M, D))
    loss = lax.psum(loss_part, ("x", "y", "c"))

    big_g = [None] * depth
    small_g = [None] * depth
    pending = None
    for i in reversed(range(depth)):
        dy, red_early, red_pending, late, small_g[i] = layer_bwd(dy, saved[i], cos, sin, Ws[i], small(i), pending)
        big_g[i] = red_early
        if pending:
            big_g[i + 1].update(zip(LATE_GRADS, red_pending))
        pending = late
    big_g[0].update(zip(LATE_GRADS, rs_finish(pending, run_rider(chips_rider(pending[0]), "rs_chip_exchange"))))

    flat_small = jnp.concatenate([small_g[i][n].reshape(-1) for i in range(depth) for n in SMALL])
    n_small = flat_small.shape[0]
    pad = (-n_small) % (SUBLANE * LANE)
    parts = all_gather([jnp.pad(flat_small, (0, pad)).reshape(-1, LANE)], "small_grads_all_gather")[0]
    tot = add_rows([parts[d] for d in range(N_DEV)], F32, "small_grads_sum").reshape(-1)[:n_small]
    grads, off = {}, 0
    per = {}
    for i in range(depth):
        for n in SMALL:
            sz = w[n].shape[1]
            per.setdefault(n, []).append(tot[off:off + sz])
            off += sz
    for n in SMALL:
        grads[n] = jnp.stack(per[n])
    for n in BIG:
        grads[n] = jnp.stack([big_g[i][n] for i in range(depth)])

    delta, new_m, new_v = {}, {}, {}
    for n in WEIGHTS:
        delta[n], new_m[n], new_v[n] = adamw(w[n], grads[n], mom[n], var[n], "adamw_" + n)
    return (loss, dy.reshape(x.shape), *[grads[n] for n in WEIGHTS], *[delta[n] for n in WEIGHTS],
            *[new_m[n] for n in WEIGHTS], *[new_v[n] for n in WEIGHTS])
```

```python
import functools
import math

import numpy as np
import jax
import jax.numpy as jnp
from jax import lax
from jax.experimental import pallas as pl
from jax.experimental.pallas import tpu as pltpu

F32 = jnp.float32
BF16 = jnp.bfloat16
MESH = pl.DeviceIdType.MESH

EPS = 1e-6
CHUNK = 64
MLA_HEADS = 16
Q_LORA = 512
KV_LORA = 512
QK_NOPE = 128
QK_ROPE = 64
V_DIM = 128
ROPE_THETA = 10000.0
SSM_HEADDIM = 64
SSM_GROUPS = 8
SSM_STATE = 128
CONV_WIDTH = 4
N_DEV = 8
ADAM_LR = 0.001
ADAM_B1 = 0.9
ADAM_B2 = 0.999
ADAM_EPS = 1e-08
ADAM_WD = 0.01
ADAM_STEP = 10

LANE = 128
SUBLANE = 8
VMEM_LIMIT = 56 * 1024 * 1024

BIG = ["w_in", "w_uq", "w_ukv", "w_o_mla", "conv_w", "w_o_ssm", "w_out", "w_up", "w_down", "w_ple_gate", "w_ple"]
ROW_SHARDED = {"w_o_mla", "w_o_ssm", "w_out", "w_down", "w_ple_gate"}
F32_ON_WIRE = {"conv_w"}
SMALL = ["norm_mix_w", "q_a_norm_w", "kv_a_norm_w", "q_norm_w", "k_norm_w", "conv_b", "dt_bias", "a_log", "d_skip",
         "ssm_norm_w", "norm_mlp_w", "ple_norm_w"]
WEIGHTS = ["norm_mix_w", "w_in", "q_a_norm_w", "w_uq", "kv_a_norm_w", "w_ukv", "q_norm_w", "k_norm_w", "w_o_mla",
           "conv_w", "conv_b", "dt_bias", "a_log", "d_skip", "ssm_norm_w", "w_o_ssm", "w_out", "norm_mlp_w", "w_up",
           "w_down", "ple_norm_w", "w_ple_gate", "w_ple"]


def _pick(dim, pref, align=LANE):
    if dim <= pref:
        return dim
    best = 0
    for t in range(align, pref + 1, align):
        if dim % t == 0:
            best = t
    return best or dim


def _params(n_axes):
    return pltpu.CompilerParams(dimension_semantics=("arbitrary",) * n_axes, vmem_limit_bytes=VMEM_LIMIT)


_DIMS = {"nn": ((1,), (0,)), "nt": ((1,), (1,)), "tn": ((0,), (0,))}


def _dg(a, b, kind):
    return lax.dot_general(a.astype(BF16), b.astype(BF16), (_DIMS[kind], ((), ())), preferred_element_type=F32)


@functools.partial(jax.custom_vjp, nondiff_argnums=(2,))
def bdot(a, b, kind):
    return _dg(a, b, kind)


def _bdot_fwd(a, b, kind):
    return _dg(a, b, kind), (a, b)


def _bdot_bwd(kind, res, g):
    a, b = res
    if kind == "nn":
        da, db = _dg(g, b, "nt"), _dg(a, g, "tn")
    elif kind == "nt":
        da, db = _dg(g, b, "nn"), _dg(g, a, "tn")
    else:
        da, db = _dg(b, g, "nt"), _dg(a, g, "nn")
    return da.astype(a.dtype), db.astype(b.dtype)


bdot.defvjp(_bdot_fwd, _bdot_bwd)


def mm(a, b, kind="nn", out_dtype=F32, add=None, name="mm", tm=1024, tn=1536, tk=2048):
    if kind == "tn":
        K, M = a.shape
    else:
        M, K = a.shape
    N = b.shape[0] if kind == "nt" else b.shape[1]
    tm, tn, tk = _pick(M, tm), _pick(N, tn), _pick(K, tk)
    nk = K // tk
    a_spec = pl.BlockSpec((tk, tm), lambda i, j, k: (k, i)) if kind == "tn" else pl.BlockSpec((tm, tk), lambda i, j, k: (i, k))
    b_spec = pl.BlockSpec((tn, tk), lambda i, j, k: (j, k)) if kind == "nt" else pl.BlockSpec((tk, tn), lambda i, j, k: (k, j))
    o_spec = pl.BlockSpec((tm, tn), lambda i, j, k: (i, j))
    has_add = add is not None

    def body(*refs):
        a_ref, b_ref = refs[0], refs[1]
        add_ref = refs[2] if has_add else None
        o_ref, acc_ref = refs[-2], refs[-1]
        k = pl.program_id(2)

        @pl.when(k == 0)
        def _():
            acc_ref[...] = jnp.zeros_like(acc_ref)

        acc_ref[...] += _dg(a_ref[...], b_ref[...], kind)

        @pl.when(k == nk - 1)
        def _():
            r = acc_ref[...]
            if has_add:
                r = r + add_ref[...].astype(F32)
            o_ref[...] = r.astype(o_ref.dtype)

    return pl.pallas_call(
        body, name=name, grid=(M // tm, N // tn, nk),
        in_specs=[a_spec, b_spec] + ([o_spec] if has_add else []), out_specs=o_spec,
        out_shape=jax.ShapeDtypeStruct((M, N), out_dtype),
        scratch_shapes=[pltpu.VMEM((tm, tn), F32)], compiler_params=_params(3),
    )(*([a, b] + ([add] if has_add else [])))


def _spec(kind, C, tile, h_outer, heads_block=None):
    def ix(f):
        return (lambda a, b: f(a, b)) if h_outer else (lambda a, b: f(b, a))

    if kind == "r":
        return pl.BlockSpec((tile, C), ix(lambda h, i: (i, 0)))
    if kind == "h":
        return pl.BlockSpec((heads_block, tile, C), ix(lambda h, i: (h, i, 0)))
    if kind == "c":
        return pl.BlockSpec((tile, C), ix(lambda h, i: (i, h)))
    if kind == "p":
        return pl.BlockSpec((1, C), ix(lambda h, i: (0, 0)))
    assert kind == "pc"
    return pl.BlockSpec((1, C), ix(lambda h, i: (0, h)))


def _shape(kind, C, M, H):
    return {"r": (M, C), "h": (H, M, C), "c": (M, H * C), "p": (1, C), "pc": (1, H * C)}[kind]


def _width(kind, arr, H):
    return arr.shape[-1] // H if kind in ("c", "pc") else arr.shape[-1]


def tile_fwd(fn, ins, kinds, outs, *, M, H=1, tile=256, all_heads=False, name):
    tile = min(tile, M)
    n = len(ins)
    grid = (M // tile, 1 if all_heads else H)
    hb = H if all_heads else None

    def body(*refs):
        res = fn(*[r[...].astype(F32) for r in refs[:n]])
        for o, r in zip(refs[n:], res):
            o[...] = r.astype(o.dtype)

    return pl.pallas_call(
        body, name=name, grid=grid,
        in_specs=[_spec(k, _width(k, a, H), tile, False, hb) for k, a in zip(kinds, ins)],
        out_specs=[_spec(k, C, tile, False, hb) for k, C, _ in outs],
        out_shape=[jax.ShapeDtypeStruct(_shape(k, C, M, H), dt) for k, C, dt in outs],
        compiler_params=_params(2),
    )(*ins)


def tile_bwd(fn, ins, kinds, cts, ct_kinds, need, *, M, H=1, tile=128, h_outer=False, all_heads=False, adds=None, gdt=None, name):
    tile = min(tile, M)
    n, nc = len(ins), len(cts)
    adds = adds or {}
    gdt = gdt or {}
    add_idx = sorted(adds)
    want = [j for j in range(n) if need[j]]
    hb = H if all_heads else None
    H = 1 if all_heads else H
    grid = (H, M // tile) if h_outer else (M // tile, H)
    for j in want:
        assert not (kinds[j] == "r" and H > 1 and h_outer) and not (kinds[j] == "pc" and not h_outer)

    def body(*refs):
        in_refs, ct_refs = refs[:n], refs[n:n + nc]
        add_refs = refs[n + nc:n + nc + len(add_idx)]
        out_refs = refs[n + nc + len(add_idx):]
        h = pl.program_id(0 if h_outer else 1)
        i = pl.program_id(1 if h_outer else 0)
        _, vjp = jax.vjp(fn, *[r[...].astype(F32) for r in in_refs])
        grads = vjp(tuple(r[...].astype(F32) for r in ct_refs))
        for o, j in zip(out_refs, want):
            g = grads[j]
            if j in adds:
                g = g + add_refs[add_idx.index(j)][...].astype(F32)
            k = kinds[j]
            if k in ("h", "c") or (k == "r" and H == 1):
                o[...] = g.astype(o.dtype)
                continue
            first = {"r": h == 0, "p": jnp.logical_and(h == 0, i == 0), "pc": i == 0}[k]

            @pl.when(first)
            def _(o=o, g=g):
                o[...] = g.astype(o.dtype)

            @pl.when(jnp.logical_not(first))
            def _(o=o, g=g):
                o[...] += g.astype(o.dtype)

    specs = lambda ks, arrs: [_spec(k, _width(k, a, H), tile, h_outer, hb) for k, a in zip(ks, arrs)]
    add_arrs = [adds[j] for j in add_idx]
    return pl.pallas_call(
        body, name=name, grid=grid,
        in_specs=specs(kinds, ins) + specs(ct_kinds, cts) + specs([kinds[j] for j in add_idx], add_arrs),
        out_specs=specs([kinds[j] for j in want], [ins[j] for j in want]),
        out_shape=[jax.ShapeDtypeStruct(ins[j].shape, gdt.get(j, F32)) for j in want],
        compiler_params=_params(2),
    )(*ins, *cts, *add_arrs)


def _rms(x, w):
    return x * lax.rsqrt(jnp.mean(x * x, axis=-1, keepdims=True) + EPS) * w


def _sigmoid(x):
    return 1.0 / (1.0 + jnp.exp(-x))


def f_norm(x, w):
    return (_rms(x, w),)


def f_mla_a(cq, ckv, wq, wkv):
    return _rms(cq, wq), _rms(ckv, wkv)


def f_qknorm(xn, x1, x2, cos, sin, wn, w1, w2):
    d = xn.shape[-1] + x1.shape[-1] + x2.shape[-1]
    ms = (jnp.sum(xn * xn, axis=-1, keepdims=True) + jnp.sum(x1 * x1, axis=-1, keepdims=True)
          + jnp.sum(x2 * x2, axis=-1, keepdims=True)) / d
    r = lax.rsqrt(ms + EPS)
    b1, b2 = x1 * r * w1, x2 * r * w2
    return xn * r * wn, b1 * cos - b2 * sin, b1 * sin + b2 * cos


def f_rope_tables(pos, inv_freq):
    ang = pos * inv_freq
    return jnp.cos(ang), jnp.sin(ang)


def f_dt(dt_raw, dt_bias, a_log):
    x = dt_raw + dt_bias
    dt = jnp.maximum(x, 0.0) + jnp.log(1.0 + jnp.exp(-jnp.abs(x)))
    return dt, dt * (-jnp.exp(a_log))


def f_gated_norm(y, z, w):
    yg = y * (z * _sigmoid(z))
    return (yg * lax.rsqrt(jnp.mean(yg * yg, axis=-1, keepdims=True) + EPS) * w,)


def f_merge(g0, g1, ya, yb):
    return (_sigmoid(g0) * ya + _sigmoid(g1) * yb,)


def f_act(u):
    r = jnp.maximum(u, 0.0)
    return (r * r,)


def f_ple(x, pe, gl):
    return (x + pe * _sigmoid(gl),)


def loss_head(y, target, tile=256):
    M, D = y.shape
    tile = min(tile, M)

    def body(y_ref, t_ref, dy_ref, loss_ref):
        e = y_ref[...] - t_ref[...]
        dy_ref[...] = e * (1.0 / D)
        part = jnp.full(loss_ref.shape, 0.5 / D * jnp.sum(e * e), F32)

        @pl.when(pl.program_id(0) == 0)
        def _():
            loss_ref[...] = part

        @pl.when(pl.program_id(0) != 0)
        def _():
            loss_ref[...] += part

    row = pl.BlockSpec((tile, D), lambda i: (i, 0))
    dy, loss = pl.pallas_call(
        body, name="loss_head", grid=(M // tile,), in_specs=[row, row],
        out_specs=[row, pl.BlockSpec((SUBLANE, LANE), lambda i: (0, 0))],
        out_shape=[jax.ShapeDtypeStruct((M, D), F32), jax.ShapeDtypeStruct((SUBLANE, LANE), F32)],
        compiler_params=_params(1),
    )(y, target)
    return dy, loss[0, 0]


def _shift_down(x, k, rows):
    return x if k == 0 else jnp.where(rows >= k, pltpu.roll(x, k, 0), 0.0)


def _shift_up(x, k, rows):
    M = x.shape[0]
    return x if k == 0 else jnp.where(rows < M - k, pltpu.roll(x, M - k, 0), 0.0)


def _conv_pre(x, w_ref, b_ref, rows):
    pre = b_ref[...] + w_ref[CONV_WIDTH - 1:CONV_WIDTH, :] * x
    for k in range(1, CONV_WIDTH):
        pre = pre + w_ref[CONV_WIDTH - 1 - k:CONV_WIDTH - k, :] * _shift_down(x, k, rows)
    return pre


def conv_fwd(x, w, b, ct=LANE):
    M, C = x.shape

    def body(x_ref, w_ref, b_ref, o_ref):
        rows = lax.broadcasted_iota(jnp.int32, (M, ct), 0)
        pre = _conv_pre(x_ref[...], w_ref, b_ref, rows)
        o_ref[...] = pre * _sigmoid(pre)

    col = pl.BlockSpec((M, ct), lambda j: (0, j))
    return pl.pallas_call(
        body, name="conv_fwd", grid=(C // ct,),
        in_specs=[col, pl.BlockSpec((CONV_WIDTH, ct), lambda j: (0, j)), pl.BlockSpec((1, ct), lambda j: (0, j))],
        out_specs=col, out_shape=jax.ShapeDtypeStruct((M, C), F32), compiler_params=_params(1),
    )(x, w, b)


def conv_bwd(x, w, b, dout, col0=0, ct=LANE):
    M, C = dout.shape
    j0 = col0 // ct
    assert col0 % ct == 0

    def body(x_ref, w_ref, b_ref, g_ref, dx_ref, dw_ref, db_ref):
        rows = lax.broadcasted_iota(jnp.int32, (M, ct), 0)
        xv = x_ref[...]
        pre = _conv_pre(xv, w_ref, b_ref, rows)
        s = _sigmoid(pre)
        dpre = g_ref[...] * (s * (1.0 + pre * (1.0 - s)))
        db_ref[...] = jnp.sum(dpre, axis=0, keepdims=True)
        dx = w_ref[CONV_WIDTH - 1:CONV_WIDTH, :] * dpre
        dw_ref[CONV_WIDTH - 1:CONV_WIDTH, :] = jnp.sum(dpre * xv, axis=0, keepdims=True)
        for k in range(1, CONV_WIDTH):
            dx = dx + w_ref[CONV_WIDTH - 1 - k:CONV_WIDTH - k, :] * _shift_up(dpre, k, rows)
            dw_ref[CONV_WIDTH - 1 - k:CONV_WIDTH - k, :] = jnp.sum(dpre * _shift_down(xv, k, rows), axis=0, keepdims=True)
        dx_ref[...] = dx.astype(dx_ref.dtype)

    col = pl.BlockSpec((M, ct), lambda j: (0, j))
    wsp = pl.BlockSpec((CONV_WIDTH, ct), lambda j: (0, j))
    bsp = pl.BlockSpec((1, ct), lambda j: (0, j))
    shifted = lambda rows: pl.BlockSpec((rows, ct), lambda j: (0, j + j0))
    return pl.pallas_call(
        body, name="conv_bwd", grid=(C // ct,), in_specs=[shifted(M), shifted(CONV_WIDTH), shifted(1), col], out_specs=[col, wsp, bsp],
        out_shape=[jax.ShapeDtypeStruct((M, C), BF16), jax.ShapeDtypeStruct((CONV_WIDTH, C), F32),
                   jax.ShapeDtypeStruct((1, C), F32)],
        compiler_params=_params(1),
    )(x, w, b, dout)


def _attn_tile(q, k, v, q0):
    tq, S = q.shape[0], k.shape[0]
    shift = int(math.log2(CHUNK))
    assert 1 << shift == CHUNK
    s = bdot(q, k, "nt") * ((QK_NOPE + QK_ROPE) ** -0.5)
    q_chunk = jnp.right_shift(q0 + lax.broadcasted_iota(jnp.int32, (tq, 1), 0), shift)
    k_chunk = jnp.right_shift(lax.broadcasted_iota(jnp.int32, (1, S), 1), shift)
    s = jnp.where(k_chunk <= q_chunk, s, -jnp.inf)
    m = lax.stop_gradient(jnp.max(s, axis=-1, keepdims=True))
    e = jnp.exp(s - m)
    return bdot(e, v, "nn") * (1.0 / jnp.sum(e, axis=-1, keepdims=True))


ATTN_BANDS = 8


def _band(S, tq):
    return max(tq, S // ATTN_BANDS)


def attn_fwd(q, k, v, tq=256, rider=None):
    H, S, Dk = q.shape
    Dv = v.shape[-1]
    tq = min(tq, S)
    band = _band(S, tq)

    def body(q_ref, k_ref, v_ref, o_ref):
        q0 = pl.program_id(1) * tq
        for b in range(S // band):
            L = (b + 1) * band

            @pl.when(lax.div(q0, band) == b)
            def _(L=L):
                o = _attn_tile(q_ref[...], k_ref[0:L, :], v_ref[0:L, :], q0)
                o_ref[...] = o.astype(o_ref.dtype)

    grid = (H, S // tq)
    body, r_in, r_ispec, r_oshape, r_ospec, r_scr = hosted(body, 3, 1, grid, rider)
    return pl.pallas_call(
        body, name="attn_fwd", grid=grid,
        in_specs=[pl.BlockSpec((None, tq, Dk), lambda h, i: (h, i, 0)), pl.BlockSpec((None, S, Dk), lambda h, i: (h, 0, 0)),
                  pl.BlockSpec((None, S, Dv), lambda h, i: (h, 0, 0))] + r_ispec,
        out_specs=[pl.BlockSpec((tq, Dv), lambda h, i: (i, h))] + r_ospec,
        out_shape=[jax.ShapeDtypeStruct((S, H * Dv), BF16)] + r_oshape, scratch_shapes=r_scr, compiler_params=_params(2),
    )(q, k, v, *r_in)


def attn_bwd(q, k, v, do, tq=128, rider=None):
    H, S, Dk = q.shape
    Dv = v.shape[-1]
    tq = min(tq, S)
    band = _band(S, tq)

    def body(q_ref, k_ref, v_ref, do_ref, dq_ref, dk_ref, dv_ref):
        i = pl.program_id(1)
        q0 = i * tq

        @pl.when(i == 0)
        def _():
            dk_ref[...] = jnp.zeros_like(dk_ref)
            dv_ref[...] = jnp.zeros_like(dv_ref)

        for b in range(S // band):
            L = (b + 1) * band

            @pl.when(lax.div(q0, band) == b)
            def _(L=L):
                fn = lambda a, b_, c: _attn_tile(a, b_, c, q0)
                _, vjp = jax.vjp(fn, q_ref[...].astype(F32), k_ref[0:L, :].astype(F32), v_ref[0:L, :].astype(F32))
                dq, dk, dv = vjp(do_ref[...].astype(F32))
                dq_ref[...] = dq
                dk_ref[0:L, :] += dk
                dv_ref[0:L, :] += dv

    qs = pl.BlockSpec((None, tq, Dk), lambda h, i: (h, i, 0))
    ks = pl.BlockSpec((None, S, Dk), lambda h, i: (h, 0, 0))
    vs = pl.BlockSpec((None, S, Dv), lambda h, i: (h, 0, 0))
    grid = (H, S // tq)
    body, r_in, r_ispec, r_oshape, r_ospec, r_scr = hosted(body, 4, 3, grid, rider)
    return pl.pallas_call(
        body, name="attn_bwd", grid=grid,
        in_specs=[qs, ks, vs, pl.BlockSpec((tq, Dv), lambda h, i: (i, h))] + r_ispec, out_specs=[qs, ks, vs] + r_ospec,
        out_shape=[jax.ShapeDtypeStruct((H, S, Dk), F32), jax.ShapeDtypeStruct((H, S, Dk), F32),
                   jax.ShapeDtypeStruct((H, S, Dv), F32)] + r_oshape,
        scratch_shapes=r_scr, compiler_params=_params(2),
    )(q, k, v, do, *r_in)


def _ssd_chunk(x, cb, Bm, Cm, a_col, a_row, dt_col, dt_row, h_prev, d_skip):
    T = x.shape[0]
    ti = lax.broadcasted_iota(jnp.int32, (T, T), 0)
    si = lax.broadcasted_iota(jnp.int32, (T, T), 1)
    tril = ti >= si
    acum_col = jnp.sum(jnp.where(tril, a_row, 0.0), axis=1, keepdims=True)
    acum_row = jnp.sum(jnp.where(ti <= si, a_col, 0.0), axis=0, keepdims=True)
    a_end = jnp.sum(a_row, axis=1, keepdims=True)
    decay = jnp.exp(jnp.where(tril, acum_col - acum_row, -jnp.inf))
    m = cb * decay * dt_row
    y = bdot(m, x, "nn") + bdot(Cm, h_prev, "nt") * jnp.exp(acum_col) + d_skip * x
    states = bdot(x * (jnp.exp(a_end - acum_col) * dt_col), Bm, "tn")
    return y, h_prev * jnp.exp(a_end) + states


def _ssd_specs(T, P, N, R, G, rev, nc):
    cc = (lambda c: nc - 1 - c) if rev else (lambda c: c)
    xs = pl.BlockSpec((R, T, P), lambda g, c: (g, cc(c), 0))
    bs = pl.BlockSpec((T, N), lambda g, c: (cc(c), (G * R * P) // N + g))
    cs = pl.BlockSpec((T, N), lambda g, c: (cc(c), (G * R * P) // N + G + g))
    col = pl.BlockSpec((R, T, 1), lambda g, c: (g, cc(c), 0))
    row = pl.BlockSpec((R, None, 1, T), lambda g, c: (g, cc(c), 0, 0))
    hs = pl.BlockSpec((R, None, P, N), lambda g, c: (g, cc(c), 0, 0))
    ds = pl.BlockSpec((R, 1, 1), lambda g, c: (g, 0, 0))
    return xs, bs, cs, col, row, hs, ds


def ssd_fwd(xh, xbc, a_col, a_row, dt_col, dt_row, d_skip, rider=None):
    Hs, M, P = xh.shape
    G, N, T = SSM_GROUPS, SSM_STATE, CHUNK
    R, nc = Hs // G, M // T
    xs, bs, cs, col, row, hs, ds = _ssd_specs(T, P, N, R, G, False, nc)

    def body(x_ref, b_ref, c_ref, ac_ref, ar_ref, dc_ref, dr_ref, d_ref, y_ref, hp_ref, h_scr):
        @pl.when(pl.program_id(1) == 0)
        def _():
            h_scr[...] = jnp.zeros_like(h_scr)

        Bm, Cm = b_ref[...], c_ref[...]
        cb = bdot(Cm, Bm, "nt")
        for r in range(R):
            hp = h_scr[r]
            hp_ref[r] = hp
            y, hn = _ssd_chunk(x_ref[r], cb, Bm, Cm, ac_ref[r], ar_ref[r], dc_ref[r], dr_ref[r], hp, d_ref[r])
            y_ref[r] = y
            h_scr[r] = hn

    body, r_in, r_ispec, r_oshape, r_ospec, r_scr = hosted(body, 8, 2, (G, nc), rider)
    return pl.pallas_call(
        body, name="ssd_fwd", grid=(G, nc), in_specs=[xs, bs, cs, col, row, col, row, ds] + r_ispec, out_specs=[xs, hs] + r_ospec,
        out_shape=[jax.ShapeDtypeStruct((Hs, M, P), F32), jax.ShapeDtypeStruct((Hs, nc, P, N), F32)] + r_oshape,
        scratch_shapes=[pltpu.VMEM((R, P, N), F32)] + r_scr, compiler_params=_params(2),
    )(xh, xbc, xbc, a_col, a_row, dt_col, dt_row, d_skip, *r_in)


def ssd_bwd(xh, xbc, a_col, a_row, dt_col, dt_row, d_skip, h_prev, dy, rider=None):
    Hs, M, P = xh.shape
    G, N, T = SSM_GROUPS, SSM_STATE, CHUNK
    R, nc = Hs // G, M // T
    xs, bs, cs, col, row, hs, ds = _ssd_specs(T, P, N, R, G, True, nc)
    gsp = pl.BlockSpec((T, N), lambda g, c: (nc - 1 - c, g))

    def body(x_ref, b_ref, c_ref, ac_ref, ar_ref, dc_ref, dr_ref, d_ref, hp_ref, dy_ref,
             dx_ref, db_ref, dcm_ref, dac_ref, dar_ref, ddc_ref, ddr_ref, dd_ref, dh_scr):
        first = pl.program_id(1) == 0

        @pl.when(first)
        def _():
            dh_scr[...] = jnp.zeros_like(dh_scr)
            dd_ref[...] = jnp.zeros_like(dd_ref)

        Bm, Cm = b_ref[...], c_ref[...]
        cb, cb_vjp = jax.vjp(lambda c, b: bdot(c, b, "nt"), Cm, Bm)
        db, dcm, dcb = jnp.zeros_like(Bm), jnp.zeros_like(Cm), jnp.zeros_like(cb)
        for r in range(R):
            _, vjp = jax.vjp(_ssd_chunk, x_ref[r], cb, Bm, Cm, ac_ref[r], ar_ref[r], dc_ref[r], dr_ref[r], hp_ref[r], d_ref[r])
            gx, gcb, gb, gc, gac, gar, gdc, gdr, ghp, gd = vjp((dy_ref[r], dh_scr[r]))
            dx_ref[r] = gx
            db, dcm, dcb = db + gb, dcm + gc, dcb + gcb
            dac_ref[r], dar_ref[r], ddc_ref[r], ddr_ref[r] = gac, gar, gdc, gdr
            dh_scr[r] = ghp
            dd_ref[r] += gd
        gc, gb = cb_vjp(dcb)
        db_ref[...] = db + gb
        dcm_ref[...] = dcm + gc

    f = lambda shape: jax.ShapeDtypeStruct(shape, F32)
    body, r_in, r_ispec, r_oshape, r_ospec, r_scr = hosted(body, 10, 8, (G, nc), rider)
    return pl.pallas_call(
        body, name="ssd_bwd", grid=(G, nc), in_specs=[xs, bs, cs, col, row, col, row, ds, hs, xs] + r_ispec,
        out_specs=[xs, gsp, gsp, col, row, col, row, ds] + r_ospec,
        out_shape=[f((Hs, M, P)), f((M, G * N)), f((M, G * N)), f((Hs, M, 1)), f((Hs, nc, 1, T)), f((Hs, M, 1)),
                   f((Hs, nc, 1, T)), f((Hs, 1, 1))] + r_oshape,
        scratch_shapes=[pltpu.VMEM((R, P, N), F32)] + r_scr, compiler_params=_params(2),
    )(xh, xbc, xbc, a_col, a_row, dt_col, dt_row, d_skip, h_prev, dy, *r_in)


def adamw(w, g, m, v, name):
    shape = w.shape
    C = shape[-1]
    R = w.size // C
    tile = _pick(R, max(SUBLANE, (1 << 19) // C), SUBLANE)
    c1 = 1.0 / (1.0 - ADAM_B1 ** ADAM_STEP)
    c2 = 1.0 / (1.0 - ADAM_B2 ** ADAM_STEP)

    def body(w_ref, g_ref, m_ref, v_ref, d_ref, nm_ref, nv_ref):
        gv = g_ref[...]
        nm = ADAM_B1 * m_ref[...] + (1.0 - ADAM_B1) * gv
        nv = ADAM_B2 * v_ref[...] + (1.0 - ADAM_B2) * (gv * gv)
        d_ref[...] = -ADAM_LR * ((nm * c1) / (jnp.sqrt(nv * c2) + ADAM_EPS) + ADAM_WD * w_ref[...])
        nm_ref[...] = nm
        nv_ref[...] = nv

    sp = pl.BlockSpec((tile, C), lambda i: (i, 0))
    outs = pl.pallas_call(
        body, name=name, grid=(R // tile,), in_specs=[sp] * 4, out_specs=[sp] * 3,
        out_shape=[jax.ShapeDtypeStruct((R, C), F32)] * 3, compiler_params=_params(1),
    )(*[t.reshape(R, C) for t in (w, g, m, v)])
    return [o.reshape(shape) for o in outs]


def add_rows(terms, out_dtype, name):
    shape = terms[0].shape
    C = shape[-1]
    R = terms[0].size // C
    tile = _pick(R, max(16, (1 << 19) // C), 16)
    n = len(terms)

    def body(*refs):
        acc = refs[0][...].astype(F32)
        for r in refs[1:n]:
            acc = acc + r[...].astype(F32)
        refs[n][...] = acc.astype(refs[n].dtype)

    sp = pl.BlockSpec((tile, C), lambda i: (i, 0))
    out = pl.pallas_call(body, name=name, grid=(R // tile,), in_specs=[sp] * n, out_specs=sp,
                         out_shape=jax.ShapeDtypeStruct((R, C), out_dtype), compiler_params=_params(1),
                         )(*[t.reshape(R, C) for t in terms])
    return out.reshape(shape)


_HBM = pl.BlockSpec(memory_space=pltpu.HBM)
_CHIP_FLIPS = ((1, 0), (0, 1), (1, 1))
AG_COPIES = 7


def _pos():
    return lax.axis_index("x"), lax.axis_index("y"), lax.axis_index("c")


def gather_rider(shards):
    n = len(shards)

    def copies(x_refs, out_refs, sems):
        send_sems, recv_sems, local_sems = sems
        x, y, c = _pos()
        me, sibling = (x, y, c), (x, y, 1 - c)
        chips = [(x ^ fx, y ^ fy) for fx, fy in _CHIP_FLIPS]

        def copy(t, k, block, to, src=None):
            rows = out_refs[t].at[4 * block[0] + 2 * block[1] + block[2]]
            return pltpu.make_async_remote_copy(
                src_ref=rows if src is None else src, dst_ref=rows, send_sem=send_sems.at[AG_COPIES * t + k],
                recv_sem=recv_sems.at[AG_COPIES * t + k], device_id=to, device_id_type=MESH)

        mine = [pltpu.make_async_copy(x_refs[t], out_refs[t].at[4 * x + 2 * y + c], local_sems.at[t]) for t in range(n)]
        first = []
        for j, chip in enumerate(chips):
            first += [copy(t, 1 + j, me, (*chip, c), src=x_refs[t]) for t in range(n)]
        first += [copy(t, 0, me, sibling, src=x_refs[t]) for t in range(n)]
        return copy, mine, first, me, sibling, chips, c

    def start(x_refs, out_refs, sems):
        _, mine, first, *_ = copies(x_refs, out_refs, sems)
        for cp in mine + first:
            cp.start()

    def finish(x_refs, out_refs, sems):
        copy, mine, first, me, sibling, chips, c = copies(x_refs, out_refs, sems)
        passed = []
        for j, chip in enumerate(chips):
            for t in range(n):
                copy(t, 1 + j, (*chip, c), me).wait_recv()
                passed.append(copy(t, 4 + j, (*chip, c), sibling))
                passed[-1].start()
        for t in range(n):
            copy(t, 0, sibling, me).wait_recv()
        for j, chip in enumerate(chips):
            for t in range(n):
                copy(t, 4 + j, (*chip, 1 - c), me).wait_recv()
        for cp in first + passed:
            cp.wait_send()
        for cp in mine:
            cp.wait()

    return dict(ins=list(shards), out_shapes=[jax.ShapeDtypeStruct((N_DEV,) + s.shape, s.dtype) for s in shards],
                scratch=[pltpu.SemaphoreType.DMA((AG_COPIES * n,)), pltpu.SemaphoreType.DMA((AG_COPIES * n,)),
                         pltpu.SemaphoreType.DMA((n,))], start=start, finish=finish)


def chips_rider(srcs):
    n = len(srcs)

    def copies(s_refs, o_refs, sems):
        send_sems, recv_sems = sems
        x, y, c = _pos()
        cps = []
        for k, (fx, fy) in enumerate(_CHIP_FLIPS):
            px, py = x ^ fx, y ^ fy
            cps += [pltpu.make_async_remote_copy(
                src_ref=s_refs[t].at[2 * px + py], dst_ref=o_refs[t].at[k], send_sem=send_sems.at[3 * t + k],
                recv_sem=recv_sems.at[3 * t + k], device_id=(px, py, c), device_id_type=MESH) for t in range(n)]
        return cps

    def start(s_refs, o_refs, sems):
        for cp in copies(s_refs, o_refs, sems):
            cp.start()

    def finish(s_refs, o_refs, sems):
        for cp in copies(s_refs, o_refs, sems):
            cp.wait()

    return dict(ins=list(srcs), out_shapes=[jax.ShapeDtypeStruct((3,) + s.shape[1:], s.dtype) for s in srcs],
                scratch=[pltpu.SemaphoreType.DMA((3 * n,)), pltpu.SemaphoreType.DMA((3 * n,))], start=start, finish=finish)


def run_rider(rider, name):
    ni = len(rider["ins"])
    no = len(rider["out_shapes"])

    def body(*refs):
        rider["start"](refs[:ni], refs[ni:ni + no], refs[ni + no:])
        rider["finish"](refs[:ni], refs[ni:ni + no], refs[ni + no:])

    return pl.pallas_call(body, name=name, out_shape=rider["out_shapes"], in_specs=[_HBM] * ni, out_specs=[_HBM] * no,
                          scratch_shapes=rider["scratch"])(*rider["ins"])


def hosted(body, n_in, n_out, grid, rider):
    if rider is None:
        return body, [], [], [], [], []
    ni, no = len(rider["ins"]), len(rider["out_shapes"])
    ns = len(rider["scratch"])

    def new_body(*refs):
        refs = list(refs)
        main_in, r_in = refs[:n_in], refs[n_in:n_in + ni]
        main_out, r_out = refs[n_in + ni:n_in + ni + n_out], refs[n_in + ni + n_out:n_in + ni + n_out + no]
        rest = refs[n_in + ni + n_out + no:]
        main_scr, r_scr = rest[:len(rest) - ns], rest[len(rest) - ns:]
        ids = [pl.program_id(a) for a in range(len(grid))]
        is_first = functools.reduce(jnp.logical_and, [i == 0 for i in ids])
        is_last = functools.reduce(jnp.logical_and, [i == g - 1 for i, g in zip(ids, grid)])

        @pl.when(is_first)
        def _():
            rider["start"](r_in, r_out, r_scr)

        body(*main_in, *main_out, *main_scr)

        @pl.when(is_last)
        def _():
            rider["finish"](r_in, r_out, r_scr)

    return new_body, rider["ins"], [_HBM] * ni, rider["out_shapes"], [_HBM] * no, rider["scratch"]


def all_gather(shards, name):
    return run_rider(gather_rider(shards), name)


def exchange_sibling(srcs, name):
    n = len(srcs)

    def body(*refs):
        s_refs, o_refs, send_sems, recv_sems = refs[:n], refs[n:2 * n], refs[2 * n], refs[2 * n + 1]
        x, y, c = _pos()
        cps = [pltpu.make_async_remote_copy(src_ref=s_refs[t], dst_ref=o_refs[t], send_sem=send_sems.at[t], recv_sem=recv_sems.at[t],
                                            device_id=(x, y, 1 - c), device_id_type=MESH) for t in range(n)]
        for cp in cps:
            cp.start()
        for cp in cps:
            cp.wait()

    return pl.pallas_call(
        body, name=name, out_shape=[jax.ShapeDtypeStruct(s.shape, s.dtype) for s in srcs], in_specs=[_HBM] * n, out_specs=[_HBM] * n,
        scratch_shapes=[pltpu.SemaphoreType.DMA((n,)), pltpu.SemaphoreType.DMA((n,))],
    )(*srcs)


def rs_pairs(blocks):
    x, y, c = _pos()
    chip = 2 * x + y
    by_core = [b.reshape((4, 2) + b.shape[1:]) for b in blocks]
    mine = [lax.dynamic_index_in_dim(b, c, 1, keepdims=False) for b in by_core]
    theirs = [lax.dynamic_index_in_dim(b, 1 - c, 1, keepdims=False).astype(BF16) for b in by_core]
    got = exchange_sibling(theirs, "rs_pair_exchange")
    pair = [add_rows([m, g], BF16, "rs_pair_sum") for m, g in zip(mine, got)]
    own = [lax.dynamic_index_in_dim(m, chip, 0, keepdims=False) for m in mine]
    own_got = [lax.dynamic_index_in_dim(g, chip, 0, keepdims=False) for g in got]
    return pair, own, own_got


def rs_finish(pairs, far):
    _, own, own_got = pairs
    return [add_rows([o, og, f[0], f[1], f[2]], F32, "rs_final_sum") for o, og, f in zip(own, own_got, far)]


def _full_from_gathered(name, blk):
    if name in ROW_SHARDED:
        return blk.reshape((-1,) + blk.shape[2:])
    return jnp.moveaxis(blk, 0, -2).reshape(blk.shape[1:-1] + (-1,))


def _blocks_from_full(name, full):
    if name in ROW_SHARDED:
        return full.reshape((N_DEV, -1) + full.shape[1:])
    return jnp.moveaxis(full.reshape(full.shape[:-1] + (N_DEV, -1)), -2, 0)


GATHER_WITH_ATTENTION = ["w_in", "w_uq", "w_o_mla"]
GATHER_WITH_SSD = [n for n in BIG if n not in GATHER_WITH_ATTENTION]
EARLY_GRADS = ["w_ple", "w_ple_gate", "w_down", "w_up", "w_out", "w_o_mla", "w_o_ssm"]
LATE_GRADS = [n for n in BIG if n not in EARLY_GRADS]


def wire_shards(shards, i, names):
    return [shards[n][i] if n in F32_ON_WIRE else shards[n][i].astype(BF16) for n in names]


def whole_weights(names, gathered):
    return {n: _full_from_gathered(n, blk) for n, blk in zip(names, gathered)}


def _heads_major(a, H):
    M = a.shape[0]
    return a.reshape(M, H, -1).transpose(1, 0, 2)


def _rows_major(a):
    H, M, _ = a.shape
    return a.transpose(1, 0, 2).reshape(M, -1)


def _col_row(a, nc):
    Hs = a.shape[1]
    t = a.T
    return t[:, :, None], t.reshape(Hs, nc, 1, -1)


def _qk_w(w):
    half = QK_ROPE // 2
    return w[None, :QK_NOPE], w[None, QK_NOPE:QK_NOPE + half], w[None, QK_NOPE + half:]


def layer_fwd(x, p_i, cos, sin, W, S, next_a=None, next_b=None):
    M, D = x.shape
    H, half = MLA_HEADS, QK_ROPE // 2
    d_inner = S["ssm_norm_w"].shape[1]
    Hs = d_inner // SSM_HEADDIM
    gn = SSM_GROUPS * SSM_STATE
    nc = M // CHUNK
    sv = {"x": x}
    h = tile_fwd(f_norm, [x, S["norm_mix_w"]], ["r", "p"], [("r", D, BF16)], M=M, name="norm_mix")[0]
    proj = mm(h, W["w_in"], name="mm_in")
    o0 = 0
    segs = []
    for wdt in (Q_LORA, KV_LORA, half, half, d_inner, d_inner + 2 * gn, Hs, D, D):
        segs.append(proj[:, o0:o0 + wdt])
        o0 += wdt
    c_q, c_kv, k1, k2, z, xbc_raw, dt_raw, g0, g1 = segs
    cqn, ckvn = tile_fwd(f_mla_a, [c_q, c_kv, S["q_a_norm_w"], S["kv_a_norm_w"]], ["r", "r", "p", "p"],
                         [("r", Q_LORA, BF16), ("r", KV_LORA, BF16)], M=M, name="mla_latent_norm")
    q3 = _heads_major(mm(cqn, W["w_uq"], name="mm_uq"), H)
    kv3 = _heads_major(mm(ckvn, W["w_ukv"], name="mm_ukv"), H)
    qn, q1, q2 = q3[..., :QK_NOPE], q3[..., QK_NOPE:QK_NOPE + half], q3[..., QK_NOPE + half:]
    kn, v = kv3[..., :QK_NOPE], kv3[..., QK_NOPE:].astype(BF16)
    hk = [("h", QK_NOPE, BF16), ("h", half, BF16), ("h", half, BF16)]
    q_in = [qn, q1, q2, cos, sin, *_qk_w(S["q_norm_w"][0])]
    k_in = [kn, k1, k2, cos, sin, *_qk_w(S["k_norm_w"][0])]
    qf = jnp.concatenate(tile_fwd(f_qknorm, q_in, ["h", "h", "h", "r", "r", "p", "p", "p"], hk, M=M, H=H, all_heads=True, name="q_norm_rope"), -1)
    kf = jnp.concatenate(tile_fwd(f_qknorm, k_in, ["h", "r", "r", "r", "r", "p", "p", "p"], hk, M=M, H=H, all_heads=True, name="k_norm_rope"), -1)
    o, *got_a = attn_fwd(qf, kf, v, rider=gather_rider(next_a) if next_a else None)
    y_a = mm(o, W["w_o_mla"], name="mm_o_mla")
    xbc = conv_fwd(xbc_raw, W["conv_w"], S["conv_b"])
    xh = _heads_major(xbc[:, :d_inner], Hs)
    dt, a = tile_fwd(f_dt, [dt_raw, S["dt_bias"], S["a_log"]], ["r", "p", "p"], [("r", Hs, F32), ("r", Hs, F32)], M=M, name="ssd_dt")
    a_col, a_row = _col_row(a, nc)
    dt_col, dt_row = _col_row(dt, nc)
    d_skip = S["d_skip"].reshape(Hs, 1, 1)
    y_h, h_prev, *got_b = ssd_fwd(xh, xbc, a_col, a_row, dt_col, dt_row, d_skip, rider=gather_rider(next_b) if next_b else None)
    y_ssd = _rows_major(y_h)
    yn = tile_fwd(f_gated_norm, [y_ssd, z, S["ssm_norm_w"]], ["c", "c", "pc"], [("c", d_inner // SSM_GROUPS, BF16)],
                  M=M, H=SSM_GROUPS, tile=1024, name="ssd_gated_norm")[0]
    y_b = mm(yn, W["w_o_ssm"], name="mm_o_ssm")
    merged = tile_fwd(f_merge, [g0, g1, y_a, y_b], ["r"] * 4, [("r", D, BF16)], M=M, name="merge")[0]
    x1 = mm(merged, W["w_out"], add=x, name="mm_out")
    h2 = tile_fwd(f_norm, [x1, S["norm_mlp_w"]], ["r", "p"], [("r", D, BF16)], M=M, name="norm_mlp")[0]
    u = mm(h2, W["w_up"], name="mm_up")
    act = tile_fwd(f_act, [u], ["r"], [("r", u.shape[1], BF16)], M=M, tile=128, name="relu2")[0]
    x2 = mm(act, W["w_down"], add=x1, name="mm_down")
    h3 = tile_fwd(f_norm, [x2, S["ple_norm_w"]], ["r", "p"], [("r", D, BF16)], M=M, name="norm_ple")[0]
    gl = mm(h3, W["w_ple_gate"], name="mm_ple_gate")
    pe = mm(p_i, W["w_ple"], name="mm_ple")
    x3 = tile_fwd(f_ple, [x2, pe, gl], ["r"] * 3, [("r", D, F32)], M=M, name="ple_merge")[0]
    sv.update(h=h, c_q=c_q, c_kv=c_kv, k1=k1, k2=k2, z=z, xbc_raw=xbc_raw, dt_raw=dt_raw, g0=g0, g1=g1, cqn=cqn, ckvn=ckvn,
              q_in=q_in, k_in=k_in, qf=qf, kf=kf, v=v, o=o, y_a=y_a, xbc=xbc, xh=xh, a_col=a_col, a_row=a_row, dt_col=dt_col,
              dt_row=dt_row, d_skip=d_skip, h_prev=h_prev, y_ssd=y_ssd, yn=yn, y_b=y_b, merged=merged, x1=x1, h2=h2, u=u,
              act=act, x2=x2, h3=h3, gl=gl, pe=pe, p_i=p_i)
    return x3, sv, got_a, got_b


def layer_bwd(dx3, sv, cos, sin, W, S, pending=None):
    M, D = dx3.shape
    H, half = MLA_HEADS, QK_ROPE // 2
    d_inner = S["ssm_norm_w"].shape[1]
    Hs = d_inner // SSM_HEADDIM
    nc = M // CHUNK
    gw, gs = {}, {}
    dx2, dpe, dgl = tile_bwd(f_ple, [sv["x2"], sv["pe"], sv["gl"]], ["r"] * 3, [dx3], ["r"], [True] * 3, M=M,
                             gdt={1: BF16, 2: BF16}, name="ple_merge_bwd")
    gw["w_ple"] = mm(sv["p_i"], dpe, "tn", name="mm_ple_dw")
    gw["w_ple_gate"] = mm(sv["h3"], dgl, "tn", name="mm_ple_gate_dw")
    dh3 = mm(dgl, W["w_ple_gate"], "nt", name="mm_ple_gate_dx")
    dx2, gs["ple_norm_w"] = tile_bwd(f_norm, [sv["x2"], S["ple_norm_w"]], ["r", "p"], [dh3], ["r"], [True, True], M=M,
                                     adds={0: dx2}, name="norm_ple_bwd")
    dx2b = dx2.astype(BF16)
    gw["w_down"] = mm(sv["act"], dx2b, "tn", name="mm_down_dw")
    dact = mm(dx2b, W["w_down"], "nt", name="mm_down_dx")
    du = tile_bwd(f_act, [sv["u"]], ["r"], [dact], ["r"], [True], M=M, tile=128, gdt={0: BF16}, name="relu2_bwd")[0]
    gw["w_up"] = mm(sv["h2"], du, "tn", name="mm_up_dw")
    dh2 = mm(du, W["w_up"], "nt", name="mm_up_dx")
    dx1, gs["norm_mlp_w"] = tile_bwd(f_norm, [sv["x1"], S["norm_mlp_w"]], ["r", "p"], [dh2], ["r"], [True, True], M=M,
                                     adds={0: dx2}, name="norm_mlp_bwd")
    dx1b = dx1.astype(BF16)
    gw["w_out"] = mm(sv["merged"], dx1b, "tn", name="mm_out_dw")
    dmerged = mm(dx1b, W["w_out"], "nt", name="mm_out_dx")
    dg0, dg1, dy_a, dy_b = tile_bwd(f_merge, [sv["g0"], sv["g1"], sv["y_a"], sv["y_b"]], ["r"] * 4, [dmerged], ["r"], [True] * 4,
                                    M=M, gdt={0: BF16, 1: BF16, 2: BF16, 3: BF16}, name="merge_bwd")
    gw["w_o_mla"] = mm(sv["o"], dy_a, "tn", name="mm_o_mla_dw")
    do = mm(dy_a, W["w_o_mla"], "nt", out_dtype=BF16, name="mm_o_mla_dx")
    gw["w_o_ssm"] = mm(sv["yn"], dy_b, "tn", name="mm_o_ssm_dw")
    early = rs_pairs([_blocks_from_full(n, gw[n]) for n in EARLY_GRADS])
    dqf, dkf, dv, *far = attn_bwd(sv["qf"], sv["kf"], sv["v"], do, rider=chips_rider(early[0]))
    red_early = dict(zip(EARLY_GRADS, rs_finish(early, far)))
    split = lambda t: [t[..., :QK_NOPE], t[..., QK_NOPE:QK_NOPE + half], t[..., QK_NOPE + half:]]
    hk = ["h", "h", "h"]
    need_q = [True, True, True, False, False, True, True, True]
    dqn, dq1, dq2, gqn, gq1, gq2 = tile_bwd(f_qknorm, sv["q_in"], ["h", "h", "h", "r", "r", "p", "p", "p"], split(dqf), hk,
                                            need_q, M=M, H=H, all_heads=True, gdt={0: BF16, 1: BF16, 2: BF16}, name="q_norm_rope_bwd")
    dkn, dk1, dk2, gkn, gk1, gk2 = tile_bwd(f_qknorm, sv["k_in"], ["h", "r", "r", "r", "r", "p", "p", "p"], split(dkf), hk,
                                            need_q, M=M, H=H, all_heads=True, gdt={0: BF16, 1: BF16, 2: BF16}, name="k_norm_rope_bwd")
    gs["q_norm_w"] = jnp.concatenate([gqn, gq1, gq2], -1)
    gs["k_norm_w"] = jnp.concatenate([gkn, gk1, gk2], -1)
    dq = _rows_major(jnp.concatenate([dqn, dq1, dq2], -1))
    dkv = _rows_major(jnp.concatenate([dkn, dv.astype(BF16)], -1))
    gw["w_uq"] = mm(sv["cqn"], dq, "tn", name="mm_uq_dw")
    gw["w_ukv"] = mm(sv["ckvn"], dkv, "tn", name="mm_ukv_dw")
    dcqn = mm(dq, W["w_uq"], "nt", name="mm_uq_dx")
    dckvn = mm(dkv, W["w_ukv"], "nt", name="mm_ukv_dx")
    dc_q, dc_kv, gs["q_a_norm_w"], gs["kv_a_norm_w"] = tile_bwd(
        f_mla_a, [sv["c_q"], sv["c_kv"], S["q_a_norm_w"], S["kv_a_norm_w"]], ["r", "r", "p", "p"], [dcqn, dckvn], ["r", "r"],
        [True] * 4, M=M, gdt={0: BF16, 1: BF16}, name="mla_latent_norm_bwd")
    dyn = mm(dy_b, W["w_o_ssm"], "nt", name="mm_o_ssm_dx")
    dy_ssd, dz, gs["ssm_norm_w"] = tile_bwd(f_gated_norm, [sv["y_ssd"], sv["z"], S["ssm_norm_w"]], ["c", "c", "pc"], [dyn], ["c"],
                                            [True] * 3, M=M, H=SSM_GROUPS, h_outer=True, tile=512, gdt={1: BF16}, name="ssd_gated_norm_bwd")
    dxh, dB, dC, dac, dar, ddc, ddr, dd, *far = ssd_bwd(sv["xh"], sv["xbc"], sv["a_col"], sv["a_row"], sv["dt_col"], sv["dt_row"],
                                                       sv["d_skip"], sv["h_prev"], _heads_major(dy_ssd, Hs),
                                                       rider=chips_rider(pending[0]) if pending else None)
    red_pending = rs_finish(pending, far) if pending else None
    gs["d_skip"] = dd.reshape(1, Hs)
    da = (dac[:, :, 0] + dar.reshape(Hs, M)).T
    ddt = (ddc[:, :, 0] + ddr.reshape(Hs, M)).T
    ddt_raw, gs["dt_bias"], gs["a_log"] = tile_bwd(f_dt, [sv["dt_raw"], S["dt_bias"], S["a_log"]], ["r", "p", "p"], [ddt, da],
                                                   ["r", "r"], [True] * 3, M=M, gdt={0: BF16}, name="ssd_dt_bwd")
    conv_parts, col0 = [], 0
    for part in (_rows_major(dxh), dB, dC):
        conv_parts.append(conv_bwd(sv["xbc_raw"], W["conv_w"], S["conv_b"], part, col0))
        col0 += part.shape[1]
    dxbc_raw, dconv_w, dconv_b = zip(*conv_parts)
    gw["conv_w"], gs["conv_b"] = jnp.concatenate(dconv_w, -1), jnp.concatenate(dconv_b, -1)
    dproj = jnp.concatenate([dc_q, dc_kv, dk1, dk2, dz, *dxbc_raw, ddt_raw, dg0, dg1], -1)
    gw["w_in"] = mm(sv["h"], dproj, "tn", name="mm_in_dw")
    dh = mm(dproj, W["w_in"], "nt", name="mm_in_dx")
    dx, gs["norm_mix_w"] = tile_bwd(f_norm, [sv["x"], S["norm_mix_w"]], ["r", "p"], [dh], ["r"], [True, True], M=M,
                                    adds={0: dx1}, name="norm_mix_bwd")
    late = rs_pairs([_blocks_from_full(n, gw[n]) for n in LATE_GRADS])
    return dx, red_early, red_pending, late, gs


def kernel(x, p, positions, norm_mix_w, w_in, q_a_norm_w, w_uq, kv_a_norm_w, w_ukv, q_norm_w, k_norm_w, w_o_mla, conv_w, conv_b, dt_bias, a_log, d_skip, ssm_norm_w, w_o_ssm, w_out, norm_mlp_w, w_up, w_down, ple_norm_w, w_ple_gate, w_ple, loss_target, m_norm_mix_w, m_w_in, m_q_a_norm_w, m_w_uq, m_kv_a_norm_w, m_w_ukv, m_q_norm_w, m_k_norm_w, m_w_o_mla, m_conv_w, m_conv_b, m_dt_bias, m_a_log, m_d_skip, m_ssm_norm_w, m_w_o_ssm, m_w_out, m_norm_mlp_w, m_w_up, m_w_down, m_ple_norm_w, m_w_ple_gate, m_w_ple, v_norm_mix_w, v_w_in, v_q_a_norm_w, v_w_uq, v_kv_a_norm_w, v_w_ukv, v_q_norm_w, v_k_norm_w, v_w_o_mla, v_conv_w, v_conv_b, v_dt_bias, v_a_log, v_d_skip, v_ssm_norm_w, v_w_o_ssm, v_w_out, v_norm_mlp_w, v_w_up, v_w_down, v_ple_norm_w, v_w_ple_gate, v_w_ple):
    env = dict(locals())
    w = {n: env[n] for n in WEIGHTS}
    mom = {n: env["m_" + n] for n in WEIGHTS}
    var = {n: env["v_" + n] for n in WEIGHTS}
    depth = w_in.shape[0]
    M, D = x.shape[1], x.shape[2]
    xs = x.reshape(M, D)
    half = QK_ROPE // 2
    inv_freq = jnp.asarray(1.0 / (ROPE_THETA ** (np.arange(0, QK_ROPE, 2, dtype=np.float32) / QK_ROPE)), F32)[None, :]
    cos, sin = tile_fwd(f_rope_tables, [positions.reshape(M, 1).astype(F32), inv_freq], ["r", "p"],
                        [("r", half, F32), ("r", half, F32)], M=M, name="rope_tables")
    shards = {n: w[n] for n in BIG}
    small = lambda i: {n: w[n][i][None, :] for n in SMALL}

    Ws, saved = [], []
    got_a = all_gather(wire_shards(shards, 0, GATHER_WITH_ATTENTION), "weights_all_gather_a")
    got_b = all_gather(wire_shards(shards, 0, GATHER_WITH_SSD), "weights_all_gather_b")
    for i in range(depth):
        Wi = {**whole_weights(GATHER_WITH_ATTENTION, got_a), **whole_weights(GATHER_WITH_SSD, got_b)}
        nxt = i + 1 < depth
        xs, sv, got_a, got_b = layer_fwd(xs, p[i].reshape(M, -1).astype(BF16), cos, sin, Wi, small(i),
                                        wire_shards(shards, i + 1, GATHER_WITH_ATTENTION) if nxt else None,
                                        wire_shards(shards, i + 1, GATHER_WITH_SSD) if nxt else None)
        Ws.append(Wi)
        saved.append(sv)
    dy, loss_part = loss_head(xs, loss_target.reshape(M, D))
    loss = lax.psum(loss_part, ("x", "y", "c"))

    big_g = [None] * depth
    small_g = [None] * depth
    pending = None
    for i in reversed(range(depth)):
        dy, red_early, red_pending, late, small_g[i] = layer_bwd(dy, saved[i], cos, sin, Ws[i], small(i), pending)
        big_g[i] = red_early
        if pending:
            big_g[i + 1].update(zip(LATE_GRADS, red_pending))
        pending = late
    big_g[0].update(zip(LATE_GRADS, rs_finish(pending, run_rider(chips_rider(pending[0]), "rs_chip_exchange"))))

    flat_small = jnp.concatenate([small_g[i][n].reshape(-1) for i in range(depth) for n in SMALL])
    n_small = flat_small.shape[0]
    pad = (-n_small) % (SUBLANE * LANE)
    parts = all_gather([jnp.pad(flat_small, (0, pad)).reshape(-1, LANE)], "small_grads_all_gather")[0]
    tot = add_rows([parts[d] for d in range(N_DEV)], F32, "small_grads_sum").reshape(-1)[:n_small]
    grads, off = {}, 0
    per = {}
    for i in range(depth):
        for n in SMALL:
            sz = w[n].shape[1]
            per.setdefault(n, []).append(tot[off:off + sz])
            off += sz
    for n in SMALL:
        grads[n] = jnp.stack(per[n])
    for n in BIG:
        grads[n] = jnp.stack([big_g[i][n] for i in range(depth)])

    delta, new_m, new_v = {}, {}, {}
    for n in WEIGHTS:
        delta[n], new_m[n], new_v[n] = adamw(w[n], grads[n], mom[n], var[n], "adamw_" + n)
    return (loss, dy.reshape(x.shape), *[grads[n] for n in WEIGHTS], *[delta[n] for n in WEIGHTS],
            *[new_m[n] for n in WEIGHTS], *[new_v[n] for n in WEIGHTS])
```

```python
import functools
import math

import numpy as np
import jax
import jax.numpy as jnp
from jax import lax
from jax.experimental import pallas as pl
from jax.experimental.pallas import tpu as pltpu

F32 = jnp.float32
BF16 = jnp.bfloat16
MESH = pl.DeviceIdType.MESH

EPS = 1e-6
CHUNK = 64
MLA_HEADS = 16
Q_LORA = 512
KV_LORA = 512
QK_NOPE = 128
QK_ROPE = 64
V_DIM = 128
ROPE_THETA = 10000.0
SSM_HEADDIM = 64
SSM_GROUPS = 8
SSM_STATE = 128
CONV_WIDTH = 4
N_DEV = 8
ADAM_LR = 0.001
ADAM_B1 = 0.9
ADAM_B2 = 0.999
ADAM_EPS = 1e-08
ADAM_WD = 0.01
ADAM_STEP = 10

LANE = 128
SUBLANE = 8
VMEM_LIMIT = 56 * 1024 * 1024

BIG = ["w_in", "w_uq", "w_ukv", "w_o_mla", "conv_w", "w_o_ssm", "w_out", "w_up", "w_down", "w_ple_gate", "w_ple"]
ROW_SHARDED = {"w_o_mla", "w_o_ssm", "w_out", "w_down", "w_ple_gate"}
F32_ON_WIRE = {"conv_w"}
SMALL = ["norm_mix_w", "q_a_norm_w", "kv_a_norm_w", "q_norm_w", "k_norm_w", "conv_b", "dt_bias", "a_log", "d_skip",
         "ssm_norm_w", "norm_mlp_w", "ple_norm_w"]
WEIGHTS = ["norm_mix_w", "w_in", "q_a_norm_w", "w_uq", "kv_a_norm_w", "w_ukv", "q_norm_w", "k_norm_w", "w_o_mla",
           "conv_w", "conv_b", "dt_bias", "a_log", "d_skip", "ssm_norm_w", "w_o_ssm", "w_out", "norm_mlp_w", "w_up",
           "w_down", "ple_norm_w", "w_ple_gate", "w_ple"]


def _pick(dim, pref, align=LANE):
    if dim <= pref:
        return dim
    best = 0
    for t in range(align, pref + 1, align):
        if dim % t == 0:
            best = t
    return best or dim


def _params(n_axes):
    return pltpu.CompilerParams(dimension_semantics=("arbitrary",) * n_axes, vmem_limit_bytes=VMEM_LIMIT)


_DIMS = {"nn": ((1,), (0,)), "nt": ((1,), (1,)), "tn": ((0,), (0,))}


def _dg(a, b, kind):
    return lax.dot_general(a.astype(BF16), b.astype(BF16), (_DIMS[kind], ((), ())), preferred_element_type=F32)


@functools.partial(jax.custom_vjp, nondiff_argnums=(2,))
def bdot(a, b, kind):
    return _dg(a, b, kind)


def _bdot_fwd(a, b, kind):
    return _dg(a, b, kind), (a, b)


def _bdot_bwd(kind, res, g):
    a, b = res
    if kind == "nn":
        da, db = _dg(g, b, "nt"), _dg(a, g, "tn")
    elif kind == "nt":
        da, db = _dg(g, b, "nn"), _dg(g, a, "tn")
    else:
        da, db = _dg(b, g, "nt"), _dg(a, g, "nn")
    return da.astype(a.dtype), db.astype(b.dtype)


bdot.defvjp(_bdot_fwd, _bdot_bwd)


def mm(a, b, kind="nn", out_dtype=F32, add=None, name="mm", tm=1024, tn=1536, tk=2048):
    if kind == "tn":
        K, M = a.shape
    else:
        M, K = a.shape
    N = b.shape[0] if kind == "nt" else b.shape[1]
    tm, tn, tk = _pick(M, tm), _pick(N, tn), _pick(K, tk)
    nk = K // tk
    a_spec = pl.BlockSpec((tk, tm), lambda i, j, k: (k, i)) if kind == "tn" else pl.BlockSpec((tm, tk), lambda i, j, k: (i, k))
    b_spec = pl.BlockSpec((tn, tk), lambda i, j, k: (j, k)) if kind == "nt" else pl.BlockSpec((tk, tn), lambda i, j, k: (k, j))
    o_spec = pl.BlockSpec((tm, tn), lambda i, j, k: (i, j))
    has_add = add is not None

    def body(*refs):
        a_ref, b_ref = refs[0], refs[1]
        add_ref = refs[2] if has_add else None
        o_ref, acc_ref = refs[-2], refs[-1]
        k = pl.program_id(2)

        @pl.when(k == 0)
        def _():
            acc_ref[...] = jnp.zeros_like(acc_ref)

        acc_ref[...] += _dg(a_ref[...], b_ref[...], kind)

        @pl.when(k == nk - 1)
        def _():
            r = acc_ref[...]
            if has_add:
                r = r + add_ref[...].astype(F32)
            o_ref[...] = r.astype(o_ref.dtype)

    return pl.pallas_call(
        body, name=name, grid=(M // tm, N // tn, nk),
        in_specs=[a_spec, b_spec] + ([o_spec] if has_add else []), out_specs=o_spec,
        out_shape=jax.ShapeDtypeStruct((M, N), out_dtype),
        scratch_shapes=[pltpu.VMEM((tm, tn), F32)], compiler_params=_params(3),
    )(*([a, b] + ([add] if has_add else [])))


def _spec(kind, C, tile, h_outer, heads_block=None):
    def ix(f):
        return (lambda a, b: f(a, b)) if h_outer else (lambda a, b: f(b, a))

    if kind == "r":
        return pl.BlockSpec((tile, C), ix(lambda h, i: (i, 0)))
    if kind == "h":
        return pl.BlockSpec((heads_block, tile, C), ix(lambda h, i: (h, i, 0)))
    if kind == "c":
        return pl.BlockSpec((tile, C), ix(lambda h, i: (i, h)))
    if kind == "p":
        return pl.BlockSpec((1, C), ix(lambda h, i: (0, 0)))
    assert kind == "pc"
    return pl.BlockSpec((1, C), ix(lambda h, i: (0, h)))


def _shape(kind, C, M, H):
    return {"r": (M, C), "h": (H, M, C), "c": (M, H * C), "p": (1, C), "pc": (1, H * C)}[kind]


def _width(kind, arr, H):
    return arr.shape[-1] // H if kind in ("c", "pc") else arr.shape[-1]


def tile_fwd(fn, ins, kinds, outs, *, M, H=1, tile=256, all_heads=False, name):
    tile = min(tile, M)
    n = len(ins)
    grid = (M // tile, 1 if all_heads else H)
    hb = H if all_heads else None

    def body(*refs):
        res = fn(*[r[...].astype(F32) for r in refs[:n]])
        for o, r in zip(refs[n:], res):
            o[...] = r.astype(o.dtype)

    return pl.pallas_call(
        body, name=name, grid=grid,
        in_specs=[_spec(k, _width(k, a, H), tile, False, hb) for k, a in zip(kinds, ins)],
        out_specs=[_spec(k, C, tile, False, hb) for k, C, _ in outs],
        out_shape=[jax.ShapeDtypeStruct(_shape(k, C, M, H), dt) for k, C, dt in outs],
        compiler_params=_params(2),
    )(*ins)


def tile_bwd(fn, ins, kinds, cts, ct_kinds, need, *, M, H=1, tile=128, h_outer=False, all_heads=False, adds=None, gdt=None, name):
    tile = min(tile, M)
    n, nc = len(ins), len(cts)
    adds = adds or {}
    gdt = gdt or {}
    add_idx = sorted(adds)
    want = [j for j in range(n) if need[j]]
    hb = H if all_heads else None
    H = 1 if all_heads else H
    grid = (H, M // tile) if h_outer else (M // tile, H)
    for j in want:
        assert not (kinds[j] == "r" and H > 1 and h_outer) and not (kinds[j] == "pc" and not h_outer)

    def body(*refs):
        in_refs, ct_refs = refs[:n], refs[n:n + nc]
        add_refs = refs[n + nc:n + nc + len(add_idx)]
        out_refs = refs[n + nc + len(add_idx):]
        h = pl.program_id(0 if h_outer else 1)
        i = pl.program_id(1 if h_outer else 0)
        _, vjp = jax.vjp(fn, *[r[...].astype(F32) for r in in_refs])
        grads = vjp(tuple(r[...].astype(F32) for r in ct_refs))
        for o, j in zip(out_refs, want):
            g = grads[j]
            if j in adds:
                g = g + add_refs[add_idx.index(j)][...].astype(F32)
            k = kinds[j]
            if k in ("h", "c") or (k == "r" and H == 1):
                o[...] = g.astype(o.dtype)
                continue
            first = {"r": h == 0, "p": jnp.logical_and(h == 0, i == 0), "pc": i == 0}[k]

            @pl.when(first)
            def _(o=o, g=g):
                o[...] = g.astype(o.dtype)

            @pl.when(jnp.logical_not(first))
            def _(o=o, g=g):
                o[...] += g.astype(o.dtype)

    specs = lambda ks, arrs: [_spec(k, _width(k, a, H), tile, h_outer, hb) for k, a in zip(ks, arrs)]
    add_arrs = [adds[j] for j in add_idx]
    return pl.pallas_call(
        body, name=name, grid=grid,
        in_specs=specs(kinds, ins) + specs(ct_kinds, cts) + specs([kinds[j] for j in add_idx], add_arrs),
        out_specs=specs([kinds[j] for j in want], [ins[j] for j in want]),
        out_shape=[jax.ShapeDtypeStruct(ins[j].shape, gdt.get(j, F32)) for j in want],
        compiler_params=_params(2),
    )(*ins, *cts, *add_arrs)


def _rms(x, w):
    return x * lax.rsqrt(jnp.mean(x * x, axis=-1, keepdims=True) + EPS) * w


def _sigmoid(x):
    return 1.0 / (1.0 + jnp.exp(-x))


def f_norm(x, w):
    return (_rms(x, w),)


def f_mla_a(cq, ckv, wq, wkv):
    return _rms(cq, wq), _rms(ckv, wkv)


def f_qknorm(xn, x1, x2, cos, sin, wn, w1, w2):
    d = xn.shape[-1] + x1.shape[-1] + x2.shape[-1]
    ms = (jnp.sum(xn * xn, axis=-1, keepdims=True) + jnp.sum(x1 * x1, axis=-1, keepdims=True)
          + jnp.sum(x2 * x2, axis=-1, keepdims=True)) / d
    r = lax.rsqrt(ms + EPS)
    b1, b2 = x1 * r * w1, x2 * r * w2
    return xn * r * wn, b1 * cos - b2 * sin, b1 * sin + b2 * cos


def f_rope_tables(pos, inv_freq):
    ang = pos * inv_freq
    return jnp.cos(ang), jnp.sin(ang)


def f_dt(dt_raw, dt_bias, a_log):
    x = dt_raw + dt_bias
    dt = jnp.maximum(x, 0.0) + jnp.log(1.0 + jnp.exp(-jnp.abs(x)))
    return dt, dt * (-jnp.exp(a_log))


def f_gated_norm(y, z, w):
    yg = y * (z * _sigmoid(z))
    return (yg * lax.rsqrt(jnp.mean(yg * yg, axis=-1, keepdims=True) + EPS) * w,)


def f_merge(g0, g1, ya, yb):
    return (_sigmoid(g0) * ya + _sigmoid(g1) * yb,)


def f_act(u):
    r = jnp.maximum(u, 0.0)
    return (r * r,)


def f_ple(x, pe, gl):
    return (x + pe * _sigmoid(gl),)


def loss_head(y, target, tile=256):
    M, D = y.shape
    tile = min(tile, M)

    def body(y_ref, t_ref, dy_ref, loss_ref):
        e = y_ref[...] - t_ref[...]
        dy_ref[...] = e * (1.0 / D)
        part = jnp.full(loss_ref.shape, 0.5 / D * jnp.sum(e * e), F32)

        @pl.when(pl.program_id(0) == 0)
        def _():
            loss_ref[...] = part

        @pl.when(pl.program_id(0) != 0)
        def _():
            loss_ref[...] += part

    row = pl.BlockSpec((tile, D), lambda i: (i, 0))
    dy, loss = pl.pallas_call(
        body, name="loss_head", grid=(M // tile,), in_specs=[row, row],
        out_specs=[row, pl.BlockSpec((SUBLANE, LANE), lambda i: (0, 0))],
        out_shape=[jax.ShapeDtypeStruct((M, D), F32), jax.ShapeDtypeStruct((SUBLANE, LANE), F32)],
        compiler_params=_params(1),
    )(y, target)
    return dy, loss[0, 0]


def _shift_down(x, k, rows):
    return x if k == 0 else jnp.where(rows >= k, pltpu.roll(x, k, 0), 0.0)


def _shift_up(x, k, rows):
    M = x.shape[0]
    return x if k == 0 else jnp.where(rows < M - k, pltpu.roll(x, M - k, 0), 0.0)


def _conv_pre(x, w_ref, b_ref, rows):
    pre = b_ref[...] + w_ref[CONV_WIDTH - 1:CONV_WIDTH, :] * x
    for k in range(1, CONV_WIDTH):
        pre = pre + w_ref[CONV_WIDTH - 1 - k:CONV_WIDTH - k, :] * _shift_down(x, k, rows)
    return pre


def conv_fwd(x, w, b, ct=LANE):
    M, C = x.shape

    def body(x_ref, w_ref, b_ref, o_ref):
        rows = lax.broadcasted_iota(jnp.int32, (M, ct), 0)
        pre = _conv_pre(x_ref[...], w_ref, b_ref, rows)
        o_ref[...] = pre * _sigmoid(pre)

    col = pl.BlockSpec((M, ct), lambda j: (0, j))
    return pl.pallas_call(
        body, name="conv_fwd", grid=(C // ct,),
        in_specs=[col, pl.BlockSpec((CONV_WIDTH, ct), lambda j: (0, j)), pl.BlockSpec((1, ct), lambda j: (0, j))],
        out_specs=col, out_shape=jax.ShapeDtypeStruct((M, C), F32), compiler_params=_params(1),
    )(x, w, b)


def conv_bwd(x, w, b, dout, col0=0, ct=LANE):
    M, C = dout.shape
    j0 = col0 // ct
    assert col0 % ct == 0

    def body(x_ref, w_ref, b_ref, g_ref, dx_ref, dw_ref, db_ref):
        rows = lax.broadcasted_iota(jnp.int32, (M, ct), 0)
        xv = x_ref[...]
        pre = _conv_pre(xv, w_ref, b_ref, rows)
        s = _sigmoid(pre)
        dpre = g_ref[...] * (s * (1.0 + pre * (1.0 - s)))
        db_ref[...] = jnp.sum(dpre, axis=0, keepdims=True)
        dx = w_ref[CONV_WIDTH - 1:CONV_WIDTH, :] * dpre
        dw_ref[CONV_WIDTH - 1:CONV_WIDTH, :] = jnp.sum(dpre * xv, axis=0, keepdims=True)
        for k in range(1, CONV_WIDTH):
            dx = dx + w_ref[CONV_WIDTH - 1 - k:CONV_WIDTH - k, :] * _shift_up(dpre, k, rows)
            dw_ref[CONV_WIDTH - 1 - k:CONV_WIDTH - k, :] = jnp.sum(dpre * _shift_down(xv, k, rows), axis=0, keepdims=True)
        dx_ref[...] = dx.astype(dx_ref.dtype)

    col = pl.BlockSpec((M, ct), lambda j: (0, j))
    wsp = pl.BlockSpec((CONV_WIDTH, ct), lambda j: (0, j))
    bsp = pl.BlockSpec((1, ct), lambda j: (0, j))
    shifted = lambda rows: pl.BlockSpec((rows, ct), lambda j: (0, j + j0))
    return pl.pallas_call(
        body, name="conv_bwd", grid=(C // ct,), in_specs=[shifted(M), shifted(CONV_WIDTH), shifted(1), col], out_specs=[col, wsp, bsp],
        out_shape=[jax.ShapeDtypeStruct((M, C), BF16), jax.ShapeDtypeStruct((CONV_WIDTH, C), F32),
                   jax.ShapeDtypeStruct((1, C), F32)],
        compiler_params=_params(1),
    )(x, w, b, dout)


def _attn_tile(q, k, v, q0):
    tq, S = q.shape[0], k.shape[0]
    shift = int(math.log2(CHUNK))
    assert 1 << shift == CHUNK
    s = bdot(q, k, "nt") * ((QK_NOPE + QK_ROPE) ** -0.5)
    q_chunk = jnp.right_shift(q0 + lax.broadcasted_iota(jnp.int32, (tq, 1), 0), shift)
    k_chunk = jnp.right_shift(lax.broadcasted_iota(jnp.int32, (1, S), 1), shift)
    s = jnp.where(k_chunk <= q_chunk, s, -jnp.inf)
    m = lax.stop_gradient(jnp.max(s, axis=-1, keepdims=True))
    e = jnp.exp(s - m)
    return bdot(e, v, "nn") * (1.0 / jnp.sum(e, axis=-1, keepdims=True))


ATTN_BANDS = 8


def _band(S, tq):
    return max(tq, S // ATTN_BANDS)


def attn_fwd(q, k, v, tq=256, rider=None):
    H, S, Dk = q.shape
    Dv = v.shape[-1]
    tq = min(tq, S)
    band = _band(S, tq)

    def body(q_ref, k_ref, v_ref, o_ref):
        q0 = pl.program_id(1) * tq
        for b in range(S // band):
            L = (b + 1) * band

            @pl.when(lax.div(q0, band) == b)
            def _(L=L):
                o = _attn_tile(q_ref[...], k_ref[0:L, :], v_ref[0:L, :], q0)
                o_ref[...] = o.astype(o_ref.dtype)

    grid = (H, S // tq)
    body, r_in, r_ispec, r_oshape, r_ospec, r_scr = hosted(body, 3, 1, grid, rider)
    return pl.pallas_call(
        body, name="attn_fwd", grid=grid,
        in_specs=[pl.BlockSpec((None, tq, Dk), lambda h, i: (h, i, 0)), pl.BlockSpec((None, S, Dk), lambda h, i: (h, 0, 0)),
                  pl.BlockSpec((None, S, Dv), lambda h, i: (h, 0, 0))] + r_ispec,
        out_specs=[pl.BlockSpec((tq, Dv), lambda h, i: (i, h))] + r_ospec,
        out_shape=[jax.ShapeDtypeStruct((S, H * Dv), BF16)] + r_oshape, scratch_shapes=r_scr, compiler_params=_params(2),
    )(q, k, v, *r_in)


def attn_bwd(q, k, v, do, tq=256, rider=None):
    H, S, Dk = q.shape
    Dv = v.shape[-1]
    tq = min(tq, S)
    band = _band(S, tq)

    def body(q_ref, k_ref, v_ref, do_ref, dq_ref, dk_ref, dv_ref):
        i = pl.program_id(1)
        q0 = i * tq

        @pl.when(i == 0)
        def _():
            dk_ref[...] = jnp.zeros_like(dk_ref)
            dv_ref[...] = jnp.zeros_like(dv_ref)

        for b in range(S // band):
            L = (b + 1) * band

            @pl.when(lax.div(q0, band) == b)
            def _(L=L):
                fn = lambda a, b_, c: _attn_tile(a, b_, c, q0)
                _, vjp = jax.vjp(fn, q_ref[...].astype(F32), k_ref[0:L, :].astype(F32), v_ref[0:L, :].astype(F32))
                dq, dk, dv = vjp(do_ref[...].astype(F32))
                dq_ref[...] = dq
                dk_ref[0:L, :] += dk
                dv_ref[0:L, :] += dv

    qs = pl.BlockSpec((None, tq, Dk), lambda h, i: (h, i, 0))
    ks = pl.BlockSpec((None, S, Dk), lambda h, i: (h, 0, 0))
    vs = pl.BlockSpec((None, S, Dv), lambda h, i: (h, 0, 0))
    grid = (H, S // tq)
    body, r_in, r_ispec, r_oshape, r_ospec, r_scr = hosted(body, 4, 3, grid, rider)
    return pl.pallas_call(
        body, name="attn_bwd", grid=grid,
        in_specs=[qs, ks, vs, pl.BlockSpec((tq, Dv), lambda h, i: (i, h))] + r_ispec, out_specs=[qs, ks, vs] + r_ospec,
        out_shape=[jax.ShapeDtypeStruct((H, S, Dk), F32), jax.ShapeDtypeStruct((H, S, Dk), F32),
                   jax.ShapeDtypeStruct((H, S, Dv), F32)] + r_oshape,
        scratch_shapes=r_scr, compiler_params=_params(2),
    )(q, k, v, do, *r_in)


def _ssd_chunk(x, cb, Bm, Cm, a_col, a_row, dt_col, dt_row, h_prev, d_skip):
    T = x.shape[0]
    ti = lax.broadcasted_iota(jnp.int32, (T, T), 0)
    si = lax.broadcasted_iota(jnp.int32, (T, T), 1)
    tril = ti >= si
    acum_col = jnp.sum(jnp.where(tril, a_row, 0.0), axis=1, keepdims=True)
    acum_row = jnp.sum(jnp.where(ti <= si, a_col, 0.0), axis=0, keepdims=True)
    a_end = jnp.sum(a_row, axis=1, keepdims=True)
    decay = jnp.exp(jnp.where(tril, acum_col - acum_row, -jnp.inf))
    m = cb * decay * dt_row
    y = bdot(m, x, "nn") + bdot(Cm, h_prev, "nt") * jnp.exp(acum_col) + d_skip * x
    states = bdot(x * (jnp.exp(a_end - acum_col) * dt_col), Bm, "tn")
    return y, h_prev * jnp.exp(a_end) + states


def _ssd_specs(T, P, N, R, G, rev, nc):
    cc = (lambda c: nc - 1 - c) if rev else (lambda c: c)
    xs = pl.BlockSpec((R, T, P), lambda g, c: (g, cc(c), 0))
    bs = pl.BlockSpec((T, N), lambda g, c: (cc(c), (G * R * P) // N + g))
    cs = pl.BlockSpec((T, N), lambda g, c: (cc(c), (G * R * P) // N + G + g))
    col = pl.BlockSpec((R, T, 1), lambda g, c: (g, cc(c), 0))
    row = pl.BlockSpec((R, None, 1, T), lambda g, c: (g, cc(c), 0, 0))
    hs = pl.BlockSpec((R, None, P, N), lambda g, c: (g, cc(c), 0, 0))
    ds = pl.BlockSpec((R, 1, 1), lambda g, c: (g, 0, 0))
    return xs, bs, cs, col, row, hs, ds


def ssd_fwd(xh, xbc, a_col, a_row, dt_col, dt_row, d_skip, rider=None):
    Hs, M, P = xh.shape
    G, N, T = SSM_GROUPS, SSM_STATE, CHUNK
    R, nc = Hs // G, M // T
    xs, bs, cs, col, row, hs, ds = _ssd_specs(T, P, N, R, G, False, nc)

    def body(x_ref, b_ref, c_ref, ac_ref, ar_ref, dc_ref, dr_ref, d_ref, y_ref, hp_ref, h_scr):
        @pl.when(pl.program_id(1) == 0)
        def _():
            h_scr[...] = jnp.zeros_like(h_scr)

        Bm, Cm = b_ref[...], c_ref[...]
        cb = bdot(Cm, Bm, "nt")
        for r in range(R):
            hp = h_scr[r]
            hp_ref[r] = hp
            y, hn = _ssd_chunk(x_ref[r], cb, Bm, Cm, ac_ref[r], ar_ref[r], dc_ref[r], dr_ref[r], hp, d_ref[r])
            y_ref[r] = y
            h_scr[r] = hn

    body, r_in, r_ispec, r_oshape, r_ospec, r_scr = hosted(body, 8, 2, (G, nc), rider)
    return pl.pallas_call(
        body, name="ssd_fwd", grid=(G, nc), in_specs=[xs, bs, cs, col, row, col, row, ds] + r_ispec, out_specs=[xs, hs] + r_ospec,
        out_shape=[jax.ShapeDtypeStruct((Hs, M, P), F32), jax.ShapeDtypeStruct((Hs, nc, P, N), F32)] + r_oshape,
        scratch_shapes=[pltpu.VMEM((R, P, N), F32)] + r_scr, compiler_params=_params(2),
    )(xh, xbc, xbc, a_col, a_row, dt_col, dt_row, d_skip, *r_in)


def ssd_bwd(xh, xbc, a_col, a_row, dt_col, dt_row, d_skip, h_prev, dy, rider=None):
    Hs, M, P = xh.shape
    G, N, T = SSM_GROUPS, SSM_STATE, CHUNK
    R, nc = Hs // G, M // T
    xs, bs, cs, col, row, hs, ds = _ssd_specs(T, P, N, R, G, True, nc)
    gsp = pl.BlockSpec((T, N), lambda g, c: (nc - 1 - c, g))

    def body(x_ref, b_ref, c_ref, ac_ref, ar_ref, dc_ref, dr_ref, d_ref, hp_ref, dy_ref,
             dx_ref, db_ref, dcm_ref, dac_ref, dar_ref, ddc_ref, ddr_ref, dd_ref, dh_scr):
        first = pl.program_id(1) == 0

        @pl.when(first)
        def _():
            dh_scr[...] = jnp.zeros_like(dh_scr)
            dd_ref[...] = jnp.zeros_like(dd_ref)

        Bm, Cm = b_ref[...], c_ref[...]
        cb, cb_vjp = jax.vjp(lambda c, b: bdot(c, b, "nt"), Cm, Bm)
        db, dcm, dcb = jnp.zeros_like(Bm), jnp.zeros_like(Cm), jnp.zeros_like(cb)
        for r in range(R):
            _, vjp = jax.vjp(_ssd_chunk, x_ref[r], cb, Bm, Cm, ac_ref[r], ar_ref[r], dc_ref[r], dr_ref[r], hp_ref[r], d_ref[r])
            gx, gcb, gb, gc, gac, gar, gdc, gdr, ghp, gd = vjp((dy_ref[r], dh_scr[r]))
            dx_ref[r] = gx
            db, dcm, dcb = db + gb, dcm + gc, dcb + gcb
            dac_ref[r], dar_ref[r], ddc_ref[r], ddr_ref[r] = gac, gar, gdc, gdr
            dh_scr[r] = ghp
            dd_ref[r] += gd
        gc, gb = cb_vjp(dcb)
        db_ref[...] = db + gb
        dcm_ref[...] = dcm + gc

    f = lambda shape: jax.ShapeDtypeStruct(shape, F32)
    body, r_in, r_ispec, r_oshape, r_ospec, r_scr = hosted(body, 10, 8, (G, nc), rider)
    return pl.pallas_call(
        body, name="ssd_bwd", grid=(G, nc), in_specs=[xs, bs, cs, col, row, col, row, ds, hs, xs] + r_ispec,
        out_specs=[xs, gsp, gsp, col, row, col, row, ds] + r_ospec,
        out_shape=[f((Hs, M, P)), f((M, G * N)), f((M, G * N)), f((Hs, M, 1)), f((Hs, nc, 1, T)), f((Hs, M, 1)),
                   f((Hs, nc, 1, T)), f((Hs, 1, 1))] + r_oshape,
        scratch_shapes=[pltpu.VMEM((R, P, N), F32)] + r_scr, compiler_params=_params(2),
    )(xh, xbc, xbc, a_col, a_row, dt_col, dt_row, d_skip, h_prev, dy, *r_in)


def adamw(w, g, m, v, name):
    shape = w.shape
    C = shape[-1]
    R = w.size // C
    tile = _pick(R, max(SUBLANE, (1 << 19) // C), SUBLANE)
    c1 = 1.0 / (1.0 - ADAM_B1 ** ADAM_STEP)
    c2 = 1.0 / (1.0 - ADAM_B2 ** ADAM_STEP)

    def body(w_ref, g_ref, m_ref, v_ref, d_ref, nm_ref, nv_ref):
        gv = g_ref[...]
        nm = ADAM_B1 * m_ref[...] + (1.0 - ADAM_B1) * gv
        nv = ADAM_B2 * v_ref[...] + (1.0 - ADAM_B2) * (gv * gv)
        d_ref[...] = -ADAM_LR * ((nm * c1) / (jnp.sqrt(nv * c2) + ADAM_EPS) + ADAM_WD * w_ref[...])
        nm_ref[...] = nm
        nv_ref[...] = nv

    sp = pl.BlockSpec((tile, C), lambda i: (i, 0))
    outs = pl.pallas_call(
        body, name=name, grid=(R // tile,), in_specs=[sp] * 4, out_specs=[sp] * 3,
        out_shape=[jax.ShapeDtypeStruct((R, C), F32)] * 3, compiler_params=_params(1),
    )(*[t.reshape(R, C) for t in (w, g, m, v)])
    return [o.reshape(shape) for o in outs]


def add_rows(terms, out_dtype, name):
    shape = terms[0].shape
    C = shape[-1]
    R = terms[0].size // C
    tile = _pick(R, max(16, (1 << 19) // C), 16)
    n = len(terms)

    def body(*refs):
        acc = refs[0][...].astype(F32)
        for r in refs[1:n]:
            acc = acc + r[...].astype(F32)
        refs[n][...] = acc.astype(refs[n].dtype)

    sp = pl.BlockSpec((tile, C), lambda i: (i, 0))
    out = pl.pallas_call(body, name=name, grid=(R // tile,), in_specs=[sp] * n, out_specs=sp,
                         out_shape=jax.ShapeDtypeStruct((R, C), out_dtype), compiler_params=_params(1),
                         )(*[t.reshape(R, C) for t in terms])
    return out.reshape(shape)


_HBM = pl.BlockSpec(memory_space=pltpu.HBM)
_CHIP_FLIPS = ((1, 0), (0, 1), (1, 1))
AG_COPIES = 7


def _pos():
    return lax.axis_index("x"), lax.axis_index("y"), lax.axis_index("c")


def gather_rider(shards):
    n = len(shards)

    def copies(x_refs, out_refs, sems):
        send_sems, recv_sems, local_sems = sems
        x, y, c = _pos()
        me, sibling = (x, y, c), (x, y, 1 - c)
        chips = [(x ^ fx, y ^ fy) for fx, fy in _CHIP_FLIPS]

        def copy(t, k, block, to, src=None):
            rows = out_refs[t].at[4 * block[0] + 2 * block[1] + block[2]]
            return pltpu.make_async_remote_copy(
                src_ref=rows if src is None else src, dst_ref=rows, send_sem=send_sems.at[AG_COPIES * t + k],
                recv_sem=recv_sems.at[AG_COPIES * t + k], device_id=to, device_id_type=MESH)

        mine = [pltpu.make_async_copy(x_refs[t], out_refs[t].at[4 * x + 2 * y + c], local_sems.at[t]) for t in range(n)]
        first = []
        for j, chip in enumerate(chips):
            first += [copy(t, 1 + j, me, (*chip, c), src=x_refs[t]) for t in range(n)]
        first += [copy(t, 0, me, sibling, src=x_refs[t]) for t in range(n)]
        return copy, mine, first, me, sibling, chips, c

    def start(x_refs, out_refs, sems):
        _, mine, first, *_ = copies(x_refs, out_refs, sems)
        for cp in mine + first:
            cp.start()

    def finish(x_refs, out_refs, sems):
        copy, mine, first, me, sibling, chips, c = copies(x_refs, out_refs, sems)
        passed = []
        for j, chip in enumerate(chips):
            for t in range(n):
                copy(t, 1 + j, (*chip, c), me).wait_recv()
                passed.append(copy(t, 4 + j, (*chip, c), sibling))
                passed[-1].start()
        for t in range(n):
            copy(t, 0, sibling, me).wait_recv()
        for j, chip in enumerate(chips):
            for t in range(n):
                copy(t, 4 + j, (*chip, 1 - c), me).wait_recv()
        for cp in first + passed:
            cp.wait_send()
        for cp in mine:
            cp.wait()

    return dict(ins=list(shards), out_shapes=[jax.ShapeDtypeStruct((N_DEV,) + s.shape, s.dtype) for s in shards],
                scratch=[pltpu.SemaphoreType.DMA((AG_COPIES * n,)), pltpu.SemaphoreType.DMA((AG_COPIES * n,)),
                         pltpu.SemaphoreType.DMA((n,))], start=start, finish=finish)


def chips_rider(srcs):
    n = len(srcs)

    def copies(s_refs, o_refs, sems):
        send_sems, recv_sems = sems
        x, y, c = _pos()
        cps = []
        for k, (fx, fy) in enumerate(_CHIP_FLIPS):
            px, py = x ^ fx, y ^ fy
            cps += [pltpu.make_async_remote_copy(
                src_ref=s_refs[t].at[2 * px + py], dst_ref=o_refs[t].at[k], send_sem=send_sems.at[3 * t + k],
                recv_sem=recv_sems.at[3 * t + k], device_id=(px, py, c), device_id_type=MESH) for t in range(n)]
        return cps

    def start(s_refs, o_refs, sems):
        for cp in copies(s_refs, o_refs, sems):
            cp.start()

    def finish(s_refs, o_refs, sems):
        for cp in copies(s_refs, o_refs, sems):
            cp.wait()

    return dict(ins=list(srcs), out_shapes=[jax.ShapeDtypeStruct((3,) + s.shape[1:], s.dtype) for s in srcs],
                scratch=[pltpu.SemaphoreType.DMA((3 * n,)), pltpu.SemaphoreType.DMA((3 * n,))], start=start, finish=finish)


def run_rider(rider, name):
    ni = len(rider["ins"])
    no = len(rider["out_shapes"])

    def body(*refs):
        rider["start"](refs[:ni], refs[ni:ni + no], refs[ni + no:])
        rider["finish"](refs[:ni], refs[ni:ni + no], refs[ni + no:])

    return pl.pallas_call(body, name=name, out_shape=rider["out_shapes"], in_specs=[_HBM] * ni, out_specs=[_HBM] * no,
                          scratch_shapes=rider["scratch"])(*rider["ins"])


def hosted(body, n_in, n_out, grid, rider):
    if rider is None:
        return body, [], [], [], [], []
    ni, no = len(rider["ins"]), len(rider["out_shapes"])
    ns = len(rider["scratch"])

    def new_body(*refs):
        refs = list(refs)
        main_in, r_in = refs[:n_in], refs[n_in:n_in + ni]
        main_out, r_out = refs[n_in + ni:n_in + ni + n_out], refs[n_in + ni + n_out:n_in + ni + n_out + no]
        rest = refs[n_in + ni + n_out + no:]
        main_scr, r_scr = rest[:len(rest) - ns], rest[len(rest) - ns:]
        ids = [pl.program_id(a) for a in range(len(grid))]
        is_first = functools.reduce(jnp.logical_and, [i == 0 for i in ids])
        is_last = functools.reduce(jnp.logical_and, [i == g - 1 for i, g in zip(ids, grid)])

        @pl.when(is_first)
        def _():
            rider["start"](r_in, r_out, r_scr)

        body(*main_in, *main_out, *main_scr)

        @pl.when(is_last)
        def _():
            rider["finish"](r_in, r_out, r_scr)

    return new_body, rider["ins"], [_HBM] * ni, rider["out_shapes"], [_HBM] * no, rider["scratch"]


def all_gather(shards, name):
    return run_rider(gather_rider(shards), name)


def exchange_sibling(srcs, name):
    n = len(srcs)

    def body(*refs):
        s_refs, o_refs, send_sems, recv_sems = refs[:n], refs[n:2 * n], refs[2 * n], refs[2 * n + 1]
        x, y, c = _pos()
        cps = [pltpu.make_async_remote_copy(src_ref=s_refs[t], dst_ref=o_refs[t], send_sem=send_sems.at[t], recv_sem=recv_sems.at[t],
                                            device_id=(x, y, 1 - c), device_id_type=MESH) for t in range(n)]
        for cp in cps:
            cp.start()
        for cp in cps:
            cp.wait()

    return pl.pallas_call(
        body, name=name, out_shape=[jax.ShapeDtypeStruct(s.shape, s.dtype) for s in srcs], in_specs=[_HBM] * n, out_specs=[_HBM] * n,
        scratch_shapes=[pltpu.SemaphoreType.DMA((n,)), pltpu.SemaphoreType.DMA((n,))],
    )(*srcs)


def rs_pairs(blocks):
    x, y, c = _pos()
    chip = 2 * x + y
    by_core = [b.reshape((4, 2) + b.shape[1:]) for b in blocks]
    mine = [lax.dynamic_index_in_dim(b, c, 1, keepdims=False) for b in by_core]
    theirs = [lax.dynamic_index_in_dim(b, 1 - c, 1, keepdims=False).astype(BF16) for b in by_core]
    got = exchange_sibling(theirs, "rs_pair_exchange")
    pair = [add_rows([m, g], BF16, "rs_pair_sum") for m, g in zip(mine, got)]
    own = [lax.dynamic_index_in_dim(m, chip, 0, keepdims=False) for m in mine]
    own_got = [lax.dynamic_index_in_dim(g, chip, 0, keepdims=False) for g in got]
    return pair, own, own_got


def rs_finish(pairs, far):
    _, own, own_got = pairs
    return [add_rows([o, og, f[0], f[1], f[2]], F32, "rs_final_sum") for o, og, f in zip(own, own_got, far)]


def _full_from_gathered(name, blk):
    if name in ROW_SHARDED:
        return blk.reshape((-1,) + blk.shape[2:])
    return jnp.moveaxis(blk, 0, -2).reshape(blk.shape[1:-1] + (-1,))


def _blocks_from_full(name, full):
    if name in ROW_SHARDED:
        return full.reshape((N_DEV, -1) + full.shape[1:])
    return jnp.moveaxis(full.reshape(full.shape[:-1] + (N_DEV, -1)), -2, 0)


GATHER_WITH_ATTENTION = ["w_in", "w_uq", "w_o_mla"]
GATHER_WITH_SSD = [n for n in BIG if n not in GATHER_WITH_ATTENTION]
EARLY_GRADS = ["w_ple", "w_ple_gate", "w_down", "w_up", "w_out", "w_o_mla", "w_o_ssm"]
LATE_GRADS = [n for n in BIG if n not in EARLY_GRADS]


def wire_shards(shards, i, names):
    return [shards[n][i] if n in F32_ON_WIRE else shards[n][i].astype(BF16) for n in names]


def whole_weights(names, gathered):
    return {n: _full_from_gathered(n, blk) for n, blk in zip(names, gathered)}


def _heads_major(a, H):
    M = a.shape[0]
    return a.reshape(M, H, -1).transpose(1, 0, 2)


def _rows_major(a):
    H, M, _ = a.shape
    return a.transpose(1, 0, 2).reshape(M, -1)


def _col_row(a, nc):
    Hs = a.shape[1]
    t = a.T
    return t[:, :, None], t.reshape(Hs, nc, 1, -1)


def _qk_w(w):
    half = QK_ROPE // 2
    return w[None, :QK_NOPE], w[None, QK_NOPE:QK_NOPE + half], w[None, QK_NOPE + half:]


def layer_fwd(x, p_i, cos, sin, W, S, next_a=None, next_b=None):
    M, D = x.shape
    H, half = MLA_HEADS, QK_ROPE // 2
    d_inner = S["ssm_norm_w"].shape[1]
    Hs = d_inner // SSM_HEADDIM
    gn = SSM_GROUPS * SSM_STATE
    nc = M // CHUNK
    sv = {"x": x}
    h = tile_fwd(f_norm, [x, S["norm_mix_w"]], ["r", "p"], [("r", D, BF16)], M=M, name="norm_mix")[0]
    proj = mm(h, W["w_in"], name="mm_in")
    o0 = 0
    segs = []
    for wdt in (Q_LORA, KV_LORA, half, half, d_inner, d_inner + 2 * gn, Hs, D, D):
        segs.append(proj[:, o0:o0 + wdt])
        o0 += wdt
    c_q, c_kv, k1, k2, z, xbc_raw, dt_raw, g0, g1 = segs
    cqn, ckvn = tile_fwd(f_mla_a, [c_q, c_kv, S["q_a_norm_w"], S["kv_a_norm_w"]], ["r", "r", "p", "p"],
                         [("r", Q_LORA, BF16), ("r", KV_LORA, BF16)], M=M, name="mla_latent_norm")
    q3 = _heads_major(mm(cqn, W["w_uq"], name="mm_uq"), H)
    kv3 = _heads_major(mm(ckvn, W["w_ukv"], name="mm_ukv"), H)
    qn, q1, q2 = q3[..., :QK_NOPE], q3[..., QK_NOPE:QK_NOPE + half], q3[..., QK_NOPE + half:]
    kn, v = kv3[..., :QK_NOPE], kv3[..., QK_NOPE:].astype(BF16)
    hk = [("h", QK_NOPE, BF16), ("h", half, BF16), ("h", half, BF16)]
    q_in = [qn, q1, q2, cos, sin, *_qk_w(S["q_norm_w"][0])]
    k_in = [kn, k1, k2, cos, sin, *_qk_w(S["k_norm_w"][0])]
    qf = jnp.concatenate(tile_fwd(f_qknorm, q_in, ["h", "h", "h", "r", "r", "p", "p", "p"], hk, M=M, H=H, all_heads=True, name="q_norm_rope"), -1)
    kf = jnp.concatenate(tile_fwd(f_qknorm, k_in, ["h", "r", "r", "r", "r", "p", "p", "p"], hk, M=M, H=H, all_heads=True, name="k_norm_rope"), -1)
    o, *got_a = attn_fwd(qf, kf, v, rider=gather_rider(next_a) if next_a else None)
    y_a = mm(o, W["w_o_mla"], name="mm_o_mla")
    xbc = conv_fwd(xbc_raw, W["conv_w"], S["conv_b"])
    xh = _heads_major(xbc[:, :d_inner], Hs)
    dt, a = tile_fwd(f_dt, [dt_raw, S["dt_bias"], S["a_log"]], ["r", "p", "p"], [("r", Hs, F32), ("r", Hs, F32)], M=M, name="ssd_dt")
    a_col, a_row = _col_row(a, nc)
    dt_col, dt_row = _col_row(dt, nc)
    d_skip = S["d_skip"].reshape(Hs, 1, 1)
    y_h, h_prev, *got_b = ssd_fwd(xh, xbc, a_col, a_row, dt_col, dt_row, d_skip, rider=gather_rider(next_b) if next_b else None)
    y_ssd = _rows_major(y_h)
    yn = tile_fwd(f_gated_norm, [y_ssd, z, S["ssm_norm_w"]], ["c", "c", "pc"], [("c", d_inner // SSM_GROUPS, BF16)],
                  M=M, H=SSM_GROUPS, tile=1024, name="ssd_gated_norm")[0]
    y_b = mm(yn, W["w_o_ssm"], name="mm_o_ssm")
    merged = tile_fwd(f_merge, [g0, g1, y_a, y_b], ["r"] * 4, [("r", D, BF16)], M=M, name="merge")[0]
    x1 = mm(merged, W["w_out"], add=x, name="mm_out")
    h2 = tile_fwd(f_norm, [x1, S["norm_mlp_w"]], ["r", "p"], [("r", D, BF16)], M=M, name="norm_mlp")[0]
    u = mm(h2, W["w_up"], name="mm_up")
    act = tile_fwd(f_act, [u], ["r"], [("r", u.shape[1], BF16)], M=M, tile=128, name="relu2")[0]
    x2 = mm(act, W["w_down"], add=x1, name="mm_down")
    h3 = tile_fwd(f_norm, [x2, S["ple_norm_w"]], ["r", "p"], [("r", D, BF16)], M=M, name="norm_ple")[0]
    gl = mm(h3, W["w_ple_gate"], name="mm_ple_gate")
    pe = mm(p_i, W["w_ple"], name="mm_ple")
    x3 = tile_fwd(f_ple, [x2, pe, gl], ["r"] * 3, [("r", D, F32)], M=M, name="ple_merge")[0]
    sv.update(h=h, c_q=c_q, c_kv=c_kv, k1=k1, k2=k2, z=z, xbc_raw=xbc_raw, dt_raw=dt_raw, g0=g0, g1=g1, cqn=cqn, ckvn=ckvn,
              q_in=q_in, k_in=k_in, qf=qf, kf=kf, v=v, o=o, y_a=y_a, xbc=xbc, xh=xh, a_col=a_col, a_row=a_row, dt_col=dt_col,
              dt_row=dt_row, d_skip=d_skip, h_prev=h_prev, y_ssd=y_ssd, yn=yn, y_b=y_b, merged=merged, x1=x1, h2=h2, u=u,
              act=act, x2=x2, h3=h3, gl=gl, pe=pe, p_i=p_i)
    return x3, sv, got_a, got_b


def layer_bwd(dx3, sv, cos, sin, W, S, pending=None):
    M, D = dx3.shape
    H, half = MLA_HEADS, QK_ROPE // 2
    d_inner = S["ssm_norm_w"].shape[1]
    Hs = d_inner // SSM_HEADDIM
    nc = M // CHUNK
    gw, gs = {}, {}
    dx2, dpe, dgl = tile_bwd(f_ple, [sv["x2"], sv["pe"], sv["gl"]], ["r"] * 3, [dx3], ["r"], [True] * 3, M=M,
                             gdt={1: BF16, 2: BF16}, name="ple_merge_bwd")
    gw["w_ple"] = mm(sv["p_i"], dpe, "tn", name="mm_ple_dw")
    gw["w_ple_gate"] = mm(sv["h3"], dgl, "tn", name="mm_ple_gate_dw")
    dh3 = mm(dgl, W["w_ple_gate"], "nt", name="mm_ple_gate_dx")
    dx2, gs["ple_norm_w"] = tile_bwd(f_norm, [sv["x2"], S["ple_norm_w"]], ["r", "p"], [dh3], ["r"], [True, True], M=M,
                                     adds={0: dx2}, name="norm_ple_bwd")
    dx2b = dx2.astype(BF16)
    gw["w_down"] = mm(sv["act"], dx2b, "tn", name="mm_down_dw")
    dact = mm(dx2b, W["w_down"], "nt", name="mm_down_dx")
    du = tile_bwd(f_act, [sv["u"]], ["r"], [dact], ["r"], [True], M=M, tile=128, gdt={0: BF16}, name="relu2_bwd")[0]
    gw["w_up"] = mm(sv["h2"], du, "tn", name="mm_up_dw")
    dh2 = mm(du, W["w_up"], "nt", name="mm_up_dx")
    dx1, gs["norm_mlp_w"] = tile_bwd(f_norm, [sv["x1"], S["norm_mlp_w"]], ["r", "p"], [dh2], ["r"], [True, True], M=M,
                                     adds={0: dx2}, name="norm_mlp_bwd")
    dx1b = dx1.astype(BF16)
    gw["w_out"] = mm(sv["merged"], dx1b, "tn", name="mm_out_dw")
    dmerged = mm(dx1b, W["w_out"], "nt", name="mm_out_dx")
    dg0, dg1, dy_a, dy_b = tile_bwd(f_merge, [sv["g0"], sv["g1"], sv["y_a"], sv["y_b"]], ["r"] * 4, [dmerged], ["r"], [True] * 4,
                                    M=M, gdt={0: BF16, 1: BF16, 2: BF16, 3: BF16}, name="merge_bwd")
    gw["w_o_mla"] = mm(sv["o"], dy_a, "tn", name="mm_o_mla_dw")
    do = mm(dy_a, W["w_o_mla"], "nt", out_dtype=BF16, name="mm_o_mla_dx")
    gw["w_o_ssm"] = mm(sv["yn"], dy_b, "tn", name="mm_o_ssm_dw")
    early = rs_pairs([_blocks_from_full(n, gw[n]) for n in EARLY_GRADS])
    dqf, dkf, dv, *far = attn_bwd(sv["qf"], sv["kf"], sv["v"], do, rider=chips_rider(early[0]))
    red_early = dict(zip(EARLY_GRADS, rs_finish(early, far)))
    split = lambda t: [t[..., :QK_NOPE], t[..., QK_NOPE:QK_NOPE + half], t[..., QK_NOPE + half:]]
    hk = ["h", "h", "h"]
    need_q = [True, True, True, False, False, True, True, True]
    dqn, dq1, dq2, gqn, gq1, gq2 = tile_bwd(f_qknorm, sv["q_in"], ["h", "h", "h", "r", "r", "p", "p", "p"], split(dqf), hk,
                                            need_q, M=M, H=H, all_heads=True, gdt={0: BF16, 1: BF16, 2: BF16}, name="q_norm_rope_bwd")
    dkn, dk1, dk2, gkn, gk1, gk2 = tile_bwd(f_qknorm, sv["k_in"], ["h", "r", "r", "r", "r", "p", "p", "p"], split(dkf), hk,
                                            need_q, M=M, H=H, all_heads=True, gdt={0: BF16, 1: BF16, 2: BF16}, name="k_norm_rope_bwd")
    gs["q_norm_w"] = jnp.concatenate([gqn, gq1, gq2], -1)
    gs["k_norm_w"] = jnp.concatenate([gkn, gk1, gk2], -1)
    dq = _rows_major(jnp.concatenate([dqn, dq1, dq2], -1))
    dkv = _rows_major(jnp.concatenate([dkn, dv.astype(BF16)], -1))
    gw["w_uq"] = mm(sv["cqn"], dq, "tn", name="mm_uq_dw")
    gw["w_ukv"] = mm(sv["ckvn"], dkv, "tn", name="mm_ukv_dw")
    dcqn = mm(dq, W["w_uq"], "nt", name="mm_uq_dx")
    dckvn = mm(dkv, W["w_ukv"], "nt", name="mm_ukv_dx")
    dc_q, dc_kv, gs["q_a_norm_w"], gs["kv_a_norm_w"] = tile_bwd(
        f_mla_a, [sv["c_q"], sv["c_kv"], S["q_a_norm_w"], S["kv_a_norm_w"]], ["r", "r", "p", "p"], [dcqn, dckvn], ["r", "r"],
        [True] * 4, M=M, gdt={0: BF16, 1: BF16}, name="mla_latent_norm_bwd")
    dyn = mm(dy_b, W["w_o_ssm"], "nt", name="mm_o_ssm_dx")
    dy_ssd, dz, gs["ssm_norm_w"] = tile_bwd(f_gated_norm, [sv["y_ssd"], sv["z"], S["ssm_norm_w"]], ["c", "c", "pc"], [dyn], ["c"],
                                            [True] * 3, M=M, H=SSM_GROUPS, h_outer=True, tile=512, gdt={1: BF16}, name="ssd_gated_norm_bwd")
    dxh, dB, dC, dac, dar, ddc, ddr, dd, *far = ssd_bwd(sv["xh"], sv["xbc"], sv["a_col"], sv["a_row"], sv["dt_col"], sv["dt_row"],
                                                       sv["d_skip"], sv["h_prev"], _heads_major(dy_ssd, Hs),
                                                       rider=chips_rider(pending[0]) if pending else None)
    red_pending = rs_finish(pending, far) if pending else None
    gs["d_skip"] = dd.reshape(1, Hs)
    da = (dac[:, :, 0] + dar.reshape(Hs, M)).T
    ddt = (ddc[:, :, 0] + ddr.reshape(Hs, M)).T
    ddt_raw, gs["dt_bias"], gs["a_log"] = tile_bwd(f_dt, [sv["dt_raw"], S["dt_bias"], S["a_log"]], ["r", "p", "p"], [ddt, da],
                                                   ["r", "r"], [True] * 3, M=M, gdt={0: BF16}, name="ssd_dt_bwd")
    conv_parts, col0 = [], 0
    for part in (_rows_major(dxh), dB, dC):
        conv_parts.append(conv_bwd(sv["xbc_raw"], W["conv_w"], S["conv_b"], part, col0))
        col0 += part.shape[1]
    dxbc_raw, dconv_w, dconv_b = zip(*conv_parts)
    gw["conv_w"], gs["conv_b"] = jnp.concatenate(dconv_w, -1), jnp.concatenate(dconv_b, -1)
    dproj = jnp.concatenate([dc_q, dc_kv, dk1, dk2, dz, *dxbc_raw, ddt_raw, dg0, dg1], -1)
    gw["w_in"] = mm(sv["h"], dproj, "tn", name="mm_in_dw")
    dh = mm(dproj, W["w_in"], "nt", name="mm_in_dx")
    dx, gs["norm_mix_w"] = tile_bwd(f_norm, [sv["x"], S["norm_mix_w"]], ["r", "p"], [dh], ["r"], [True, True], M=M,
                                    adds={0: dx1}, name="norm_mix_bwd")
    late = rs_pairs([_blocks_from_full(n, gw[n]) for n in LATE_GRADS])
    return dx, red_early, red_pending, late, gs


def kernel(x, p, positions, norm_mix_w, w_in, q_a_norm_w, w_uq, kv_a_norm_w, w_ukv, q_norm_w, k_norm_w, w_o_mla, conv_w, conv_b, dt_bias, a_log, d_skip, ssm_norm_w, w_o_ssm, w_out, norm_mlp_w, w_up, w_down, ple_norm_w, w_ple_gate, w_ple, loss_target, m_norm_mix_w, m_w_in, m_q_a_norm_w, m_w_uq, m_kv_a_norm_w, m_w_ukv, m_q_norm_w, m_k_norm_w, m_w_o_mla, m_conv_w, m_conv_b, m_dt_bias, m_a_log, m_d_skip, m_ssm_norm_w, m_w_o_ssm, m_w_out, m_norm_mlp_w, m_w_up, m_w_down, m_ple_norm_w, m_w_ple_gate, m_w_ple, v_norm_mix_w, v_w_in, v_q_a_norm_w, v_w_uq, v_kv_a_norm_w, v_w_ukv, v_q_norm_w, v_k_norm_w, v_w_o_mla, v_conv_w, v_conv_b, v_dt_bias, v_a_log, v_d_skip, v_ssm_norm_w, v_w_o_ssm, v_w_out, v_norm_mlp_w, v_w_up, v_w_down, v_ple_norm_w, v_w_ple_gate, v_w_ple):
    env = dict(locals())
    w = {n: env[n] for n in WEIGHTS}
    mom = {n: env["m_" + n] for n in WEIGHTS}
    var = {n: env["v_" + n] for n in WEIGHTS}
    depth = w_in.shape[0]
    M, D = x.shape[1], x.shape[2]
    xs = x.reshape(M, D)
    half = QK_ROPE // 2
    inv_freq = jnp.asarray(1.0 / (ROPE_THETA ** (np.arange(0, QK_ROPE, 2, dtype=np.float32) / QK_ROPE)), F32)[None, :]
    cos, sin = tile_fwd(f_rope_tables, [positions.reshape(M, 1).astype(F32), inv_freq], ["r", "p"],
                        [("r", half, F32), ("r", half, F32)], M=M, name="rope_tables")
    shards = {n: w[n] for n in BIG}
    small = lambda i: {n: w[n][i][None, :] for n in SMALL}

    Ws, saved = [], []
    got_a = all_gather(wire_shards(shards, 0, GATHER_WITH_ATTENTION), "weights_all_gather_a")
    got_b = all_gather(wire_shards(shards, 0, GATHER_WITH_SSD), "weights_all_gather_b")
    for i in range(depth):
        Wi = {**whole_weights(GATHER_WITH_ATTENTION, got_a), **whole_weights(GATHER_WITH_SSD, got_b)}
        nxt = i + 1 < depth
        xs, sv, got_a, got_b = layer_fwd(xs, p[i].reshape(M, -1).astype(BF16), cos, sin, Wi, small(i),
                                        wire_shards(shards, i + 1, GATHER_WITH_ATTENTION) if nxt else None,
                                        wire_shards(shards, i + 1, GATHER_WITH_SSD) if nxt else None)
        Ws.append(Wi)
        saved.append(sv)
    dy, loss_part = loss_head(xs, loss_target.reshape(M, D))
    loss = lax.psum(loss_part, ("x", "y", "c"))

    big_g = [None] * depth
    small_g = [None] * depth
    pending = None
    for i in reversed(range(depth)):
        dy, red_early, red_pending, late, small_g[i] = layer_bwd(dy, saved[i], cos, sin, Ws[i], small(i), pending)
        big_g[i] = red_early
        if pending:
            big_g[i + 1].update(zip(LATE_GRADS, red_pending))
        pending = late
    big_g[0].update(zip(LATE_GRADS, rs_finish(pending, run_rider(chips_rider(pending[0]), "rs_chip_exchange"))))

    flat_small = jnp.concatenate([small_g[i][n].reshape(-1) for i in range(depth) for n in SMALL])
    n_small = flat_small.shape[0]
    pad = (-n_small) % (SUBLANE * LANE)
    parts = all_gather([jnp.pad(flat_small, (0, pad)).reshape(-1, LANE)], "small_grads_all_gather")[0]
    tot = add_rows([parts[d] for d in range(N_DEV)], F32, "small_grads_sum").reshape(-1)[:n_small]
    grads, off = {}, 0
    per = {}
    for i in range(depth):
        for n in SMALL:
            sz = w[n].shape[1]
            per.setdefault(n, []).append(tot[off:off + sz])
            off += sz
    for n in SMALL:
        grads[n] = jnp.stack(per[n])
    for n in BIG:
        grads[n] = jnp.stack([big_g[i][n] for i in range(depth)])

    delta, new_m, new_v = {}, {}, {}
    for n in WEIGHTS:
        delta[n], new_m[n], new_v[n] = adamw(w[n], grads[n], mom[n], var[n], "adamw_" + n)
    return (loss, dy.reshape(x.shape), *[grads[n] for n in WEIGHTS], *[delta[n] for n in WEIGHTS],
            *[new_m[n] for n in WEIGHTS], *[new_v[n] for n in WEIGHTS])
```
